```python
import math
import jax, jax.numpy as jnp
from jax import lax
import numpy as np


D_MODEL = 4096
BATCH = 4
SEQ = 2048
DEPTH = 2
DEC_BATCH = 8
DEC_SEQ = 8
PAST_LEN = 16384
PAGE_SIZE = 128

FOX_WIDTH = D_MODEL // 2
FOX_HEAD_DIM = 128
FOX_HEADS = FOX_WIDTH // FOX_HEAD_DIM
S5_WIDTH = D_MODEL // 2
S5_GROUP = 16
S5_GROUPS = S5_WIDTH // S5_GROUP
S5_STATE = 64
CONV_WIDTH = 31
FFN_HIDDEN = ((8 * D_MODEL + 3 * 256 - 1) // (3 * 256)) * 256
IN_COLS = 3 * FOX_WIDTH + FOX_HEADS + S5_WIDTH
Q_BLOCK = 128
N_EVEN = (DEPTH + 1) // 2
N_ODD = DEPTH // 2
RMS_EPS = 1e-6
LN_EPS = 1e-5
ATTN_SCALE = FOX_HEAD_DIM ** -0.5

kernel_name = 'hybrid_fox_s5_conformer_step'


def rmsnorm(x, g):
    xf = x.astype(jnp.float32)
    y = xf * lax.rsqrt(jnp.mean(jnp.square(xf), axis=-1, keepdims=True) + RMS_EPS)
    return (y * g.astype(jnp.float32)).astype(x.dtype)


def layernorm(x, g, b):
    xf = x.astype(jnp.float32)
    mu = jnp.mean(xf, axis=-1, keepdims=True)
    var = jnp.mean(jnp.square(xf - mu), axis=-1, keepdims=True)
    y = (xf - mu) * lax.rsqrt(var + LN_EPS) * g.astype(jnp.float32) + b.astype(jnp.float32)
    return y.astype(x.dtype)


def modulation(c, w, b):
    m = jax.nn.silu(c) @ w + b
    return jnp.split(m[:, None, :], 6, axis=-1)


def modulate(h, shift, scale):
    return h * (1.0 + scale) + shift


def swiglu(h, w_gate, w_up, w_down):
    return (jax.nn.silu(h @ w_gate) * (h @ w_up)) @ w_down


def even_projection(h, w_in, b_f):
    bsz, L, _ = h.shape
    z = h @ w_in
    q = z[..., :FOX_WIDTH].reshape(bsz, L, FOX_HEADS, FOX_HEAD_DIM)
    k = z[..., FOX_WIDTH:2 * FOX_WIDTH].reshape(bsz, L, FOX_HEADS, FOX_HEAD_DIM)
    v = z[..., 2 * FOX_WIDTH:3 * FOX_WIDTH].reshape(bsz, L, FOX_HEADS, FOX_HEAD_DIM)
    f0 = 3 * FOX_WIDTH
    logf = jax.nn.log_sigmoid((z[..., f0:f0 + FOX_HEADS] + b_f).astype(jnp.float32))
    u = z[..., f0 + FOX_HEADS:].reshape(bsz, L, S5_GROUPS, S5_GROUP)
    return q, k, v, logf, u


def fox_prompt(q, k, v, logf):
    bsz, L, H, Dh = q.shape
    cum = jnp.cumsum(logf, axis=1)
    cum_h = cum.transpose(0, 2, 1)
    nb = L // Q_BLOCK
    qb = q.reshape(bsz, nb, Q_BLOCK, H, Dh).transpose(1, 0, 2, 3, 4)
    cb = cum_h.reshape(bsz, H, nb, Q_BLOCK).transpose(2, 0, 1, 3)
    kpos = jnp.arange(L)

    def block(args):
        qi, ci, i = args
        qpos = i * Q_BLOCK + jnp.arange(Q_BLOCK)
        s = jnp.einsum('bqhd,bkhd->bhqk', qi, k, preferred_element_type=jnp.float32) * ATTN_SCALE
        s = s + ci[..., :, None] - cum_h[:, :, None, :]
        s = jnp.where(kpos[None, :] <= qpos[:, None], s, -jnp.inf)
        p = jax.nn.softmax(s, axis=-1).astype(v.dtype)
        return jnp.einsum('bhqk,bkhd->bqhd', p, v)

    out = lax.map(block, (qb, cb, jnp.arange(nb)))
    return out.transpose(1, 0, 2, 3, 4).reshape(bsz, L, H * Dh)


def fox_sample(q, k, v, logf, k_past, v_past, logf_past):
    bsz, S = q.shape[0], q.shape[1]
    P = k_past.shape[1]
    cum_past = jnp.cumsum(logf_past.astype(jnp.float32), axis=1)
    cum_new = cum_past[:, -1:, :] + jnp.cumsum(logf, axis=1)
    cp = cum_past.transpose(0, 2, 1)
    cn = cum_new.transpose(0, 2, 1)
    s_past = jnp.einsum('bqhd,bkhd->bhqk', q, k_past, preferred_element_type=jnp.float32) * ATTN_SCALE
    s_past = s_past + cn[..., :, None] - cp[..., None, :]
    s_new = jnp.einsum('bqhd,bkhd->bhqk', q, k, preferred_element_type=jnp.float32) * ATTN_SCALE
    s_new = s_new + cn[..., :, None] - cn[..., None, :]
    causal = jnp.tril(jnp.ones((S, S), dtype=bool))
    s_new = jnp.where(causal, s_new, -jnp.inf)
    p = jax.nn.softmax(jnp.concatenate([s_past, s_new], axis=-1), axis=-1).astype(v.dtype)
    out = (jnp.einsum('bhqk,bkhd->bqhd', p[..., :P], v_past)
           + jnp.einsum('bhqk,bkhd->bqhd', p[..., P:], v))
    return out.reshape(bsz, S, FOX_WIDTH)


def gather_pages(cache, layer, page_table):
    g = cache[layer, page_table]
    return g.reshape((g.shape[0], g.shape[1] * g.shape[2]) + g.shape[3:])


def _complex_affine_combine(e1, e2):
    a1r, a1i, b1r, b1i = e1
    a2r, a2i, b2r, b2i = e2
    return (a2r * a1r - a2i * a1i,
            a2r * a1i + a2i * a1r,
            a2r * b1r - a2i * b1i + b2r,
            a2r * b1i + a2i * b1r + b2i)


def s5_mix(u, h0_re, h0_im, lam_re, lam_im, log_step, b_re, b_im, c_re, c_im, d, glu_w, glu_b):
    bsz, L = u.shape[0], u.shape[1]
    uf = u.astype(jnp.float32)
    lre = lam_re.astype(jnp.float32)
    lim = lam_im.astype(jnp.float32)
    dt = jnp.exp(log_step.astype(jnp.float32))[:, None]
    mag = jnp.exp(lre * dt)
    ab_re = mag * jnp.cos(lim * dt)
    ab_im = mag * jnp.sin(lim * dt)
    nr, ni = ab_re - 1.0, ab_im
    den = lre * lre + lim * lim
    f_re = (nr * lre + ni * lim) / den
    f_im = (ni * lre - nr * lim) / den
    bb_re = f_re[..., None] * b_re - f_im[..., None] * b_im
    bb_im = f_re[..., None] * b_im + f_im[..., None] * b_re
    bu_re = jnp.einsum('blgc,gpc->blgp', uf, bb_re)
    bu_im = jnp.einsum('blgc,gpc->blgp', uf, bb_im)
    h0r = h0_re.astype(jnp.float32)
    h0i = h0_im.astype(jnp.float32)
    bu_re = bu_re.at[:, 0].add(ab_re * h0r - ab_im * h0i)
    bu_im = bu_im.at[:, 0].add(ab_re * h0i + ab_im * h0r)
    a_re = jnp.broadcast_to(ab_re, bu_re.shape)
    a_im = jnp.broadcast_to(ab_im, bu_im.shape)
    _, _, hr, hi = lax.associative_scan(_complex_affine_combine, (a_re, a_im, bu_re, bu_im), axis=1)
    y = (jnp.einsum('blgp,gcp->blgc', hr, c_re) - jnp.einsum('blgp,gcp->blgc', hi, c_im)
         + d * uf)
    y = jax.nn.gelu(y)
    y = y * jax.nn.sigmoid(jnp.einsum('blgc,gce->blge', y, glu_w) + glu_b)
    return y.reshape(bsz, L, S5_WIDTH).astype(u.dtype), hr[:, -1], hi[:, -1]


def conv_module(h, state, w_in, b_in, dw, dw_b, ln_g, ln_b, w_out):
    a = h @ w_in + b_in
    u = a[..., :D_MODEL] * jax.nn.sigmoid(a[..., D_MODEL:])
    up = jnp.concatenate([state.astype(u.dtype), u], axis=1)
    y = lax.conv_general_dilated(up, dw[:, None, :].astype(u.dtype), window_strides=(1,),
                                 padding='VALID', dimension_numbers=('NWC', 'WIO', 'NWC'),
                                 feature_group_count=D_MODEL) + dw_b
    y = jax.nn.silu(layernorm(y, ln_g, ln_b))
    return y @ w_out, up[:, -(CONV_WIDTH - 1):]


def setup_inputs(seed: int = 0) -> dict:
    key = jax.random.key(seed)
    ks = iter(jax.random.split(key, 48))
    D = D_MODEL
    n_pages = PAST_LEN // PAGE_SIZE
    n_pool = (5 * DEC_BATCH * n_pages + 3) // 4
    f32 = jnp.float32

    def nrm(shape, scale=1.0):
        return jax.random.normal(next(ks), shape, f32) * scale

    def gain(shape):
        return 1.0 + nrm(shape, 0.05)

    x_prompt = nrm((BATCH, SEQ, D))
    x_sample = nrm((DEC_BATCH, DEC_SEQ, D))
    cache_k = nrm((N_EVEN, n_pool, PAGE_SIZE, FOX_HEADS, FOX_HEAD_DIM))
    cache_v = nrm((N_EVEN, n_pool, PAGE_SIZE, FOX_HEADS, FOX_HEAD_DIM))
    cache_logf = jax.nn.log_sigmoid(3.0 + nrm((N_EVEN, n_pool, PAGE_SIZE, FOX_HEADS)))
    state_s5_re = nrm((N_EVEN, DEC_BATCH, S5_GROUPS, S5_STATE), 0.5)
    state_s5_im = nrm((N_EVEN, DEC_BATCH, S5_GROUPS, S5_STATE), 0.5)
    state_conv = nrm((N_ODD, DEC_BATCH, CONV_WIDTH - 1, D), 0.5)
    page_table = jax.random.permutation(next(ks), n_pool)[:DEC_BATCH * n_pages]
    page_table = page_table.reshape(DEC_BATCH, n_pages).astype(jnp.int32)
    c_prompt = nrm((BATCH, D))
    c_sample = nrm((DEC_BATCH, D))

    mod_w = nrm((DEPTH, D, 6 * D), 0.5 * D ** -0.5)
    mod_b = nrm((DEPTH, 6 * D), 0.02)
    norm_mix_pre = gain((DEPTH, D))
    norm_mix_post = gain((DEPTH, D))
    norm_ffn_pre = gain((DEPTH, D))
    norm_ffn_post = gain((DEPTH, D))
    ffn_w_gate = nrm((DEPTH, D, FFN_HIDDEN), D ** -0.5)
    ffn_w_up = nrm((DEPTH, D, FFN_HIDDEN), D ** -0.5)
    ffn_w_down = nrm((DEPTH, FFN_HIDDEN, D), FFN_HIDDEN ** -0.5)

    hyb_w_in = nrm((N_EVEN, D, IN_COLS), D ** -0.5)
    hyb_b_f = jax.random.uniform(next(ks), (N_EVEN, FOX_HEADS), f32, 1.0, 5.0)
    hyb_w_out = nrm((N_EVEN, FOX_WIDTH + S5_WIDTH, D), (FOX_WIDTH + S5_WIDTH) ** -0.5)
    s5_lambda_re = -0.5 + nrm((N_EVEN, S5_GROUPS, S5_STATE), 0.01)
    s5_lambda_im = jnp.pi * jnp.arange(S5_STATE, dtype=f32) + nrm((N_EVEN, S5_GROUPS, S5_STATE), 0.01)
    s5_log_step = jax.random.uniform(next(ks), (N_EVEN, S5_GROUPS), f32, math.log(1e-3), math.log(1e-1))
    s5_b_re = nrm((N_EVEN, S5_GROUPS, S5_STATE, S5_GROUP), (2 * S5_GROUP) ** -0.5)
    s5_b_im = nrm((N_EVEN, S5_GROUPS, S5_STATE, S5_GROUP), (2 * S5_GROUP) ** -0.5)
    s5_c_re = nrm((N_EVEN, S5_GROUPS, S5_GROUP, S5_STATE), S5_STATE ** -0.5)
    s5_c_im = nrm((N_EVEN, S5_GROUPS, S5_GROUP, S5_STATE), S5_STATE ** -0.5)
    s5_d = nrm((N_EVEN, S5_GROUPS, S5_GROUP))
    s5_glu_w = nrm((N_EVEN, S5_GROUPS, S5_GROUP, S5_GROUP), S5_GROUP ** -0.5)
    s5_glu_b = nrm((N_EVEN, S5_GROUPS, S5_GROUP), 0.02)

    conv_w_in = nrm((N_ODD, D, 2 * D), D ** -0.5)
    conv_b_in = nrm((N_ODD, 2 * D), 0.02)
    conv_dw = nrm((N_ODD, CONV_WIDTH, D), CONV_WIDTH ** -0.5)
    conv_dw_b = nrm((N_ODD, D), 0.02)
    conv_ln_g = gain((N_ODD, D))
    conv_ln_b = nrm((N_ODD, D), 0.02)
    conv_w_out = nrm((N_ODD, D, D), D ** -0.5)

    return {
        'x_prompt': x_prompt, 'x_sample': x_sample,
        'cache_k': cache_k, 'cache_v': cache_v, 'cache_logf': cache_logf,
        'state_s5_re': state_s5_re, 'state_s5_im': state_s5_im, 'state_conv': state_conv,
        'page_table': page_table, 'c_prompt': c_prompt, 'c_sample': c_sample,
        'mod_w': mod_w, 'mod_b': mod_b,
        'norm_mix_pre': norm_mix_pre, 'norm_mix_post': norm_mix_post,
        'norm_ffn_pre': norm_ffn_pre, 'norm_ffn_post': norm_ffn_post,
        'ffn_w_gate': ffn_w_gate, 'ffn_w_up': ffn_w_up, 'ffn_w_down': ffn_w_down,
        'hyb_w_in': hyb_w_in, 'hyb_b_f': hyb_b_f, 'hyb_w_out': hyb_w_out,
        's5_lambda_re': s5_lambda_re, 's5_lambda_im': s5_lambda_im, 's5_log_step': s5_log_step,
        's5_b_re': s5_b_re, 's5_b_im': s5_b_im, 's5_c_re': s5_c_re, 's5_c_im': s5_c_im,
        's5_d': s5_d, 's5_glu_w': s5_glu_w, 's5_glu_b': s5_glu_b,
        'conv_w_in': conv_w_in, 'conv_b_in': conv_b_in, 'conv_dw': conv_dw, 'conv_dw_b': conv_dw_b,
        'conv_ln_g': conv_ln_g, 'conv_ln_b': conv_ln_b, 'conv_w_out': conv_w_out,
    }


def reference(x_prompt, x_sample, cache_k, cache_v, cache_logf, state_s5_re, state_s5_im,
              state_conv, page_table, c_prompt, c_sample, mod_w, mod_b,
              norm_mix_pre, norm_mix_post, norm_ffn_pre, norm_ffn_post,
              ffn_w_gate, ffn_w_up, ffn_w_down, hyb_w_in, hyb_b_f, hyb_w_out,
              s5_lambda_re, s5_lambda_im, s5_log_step, s5_b_re, s5_b_im, s5_c_re, s5_c_im,
              s5_d, s5_glu_w, s5_glu_b, conv_w_in, conv_b_in, conv_dw, conv_dw_b,
              conv_ln_g, conv_ln_b, conv_w_out):
    xp, xs = x_prompt, x_sample
    bp, bs = x_prompt.shape[0], x_sample.shape[0]
    kp_l, vp_l, fp_l, ks_l, vs_l, fs_l = [], [], [], [], [], []
    s5rp_l, s5ip_l, s5rs_l, s5is_l = [], [], [], []
    cvp_l, cvs_l = [], []
    for i in range(DEPTH):
        shp1, scp1, gtp1, shp2, scp2, gtp2 = modulation(c_prompt, mod_w[i], mod_b[i])
        shs1, scs1, gts1, shs2, scs2, gts2 = modulation(c_sample, mod_w[i], mod_b[i])
        hp = modulate(rmsnorm(xp, norm_mix_pre[i]), shp1, scp1)
        hs = modulate(rmsnorm(xs, norm_mix_pre[i]), shs1, scs1)
        j = i // 2
        if i % 2 == 0:
            s5p = (s5_lambda_re[j], s5_lambda_im[j], s5_log_step[j], s5_b_re[j], s5_b_im[j],
                   s5_c_re[j], s5_c_im[j], s5_d[j], s5_glu_w[j], s5_glu_b[j])
            qp, kp, vp, fp, up = even_projection(hp, hyb_w_in[j], hyb_b_f[j])
            qs, ks, vs, fs, us = even_projection(hs, hyb_w_in[j], hyb_b_f[j])
            att_p = fox_prompt(qp, kp, vp, fp)
            att_s = fox_sample(qs, ks, vs, fs,
                               gather_pages(cache_k, j, page_table),
                               gather_pages(cache_v, j, page_table),
                               gather_pages(cache_logf, j, page_table))
            zeros_h = jnp.zeros((bp, S5_GROUPS, S5_STATE), jnp.float32)
            ssm_p, hrp, hip = s5_mix(up, zeros_h, zeros_h, *s5p)
            ssm_s, hrs, his = s5_mix(us, state_s5_re[j], state_s5_im[j], *s5p)
            op = jnp.concatenate([att_p, ssm_p], axis=-1) @ hyb_w_out[j]
            os_ = jnp.concatenate([att_s, ssm_s], axis=-1) @ hyb_w_out[j]
            kp_l.append(kp); vp_l.append(vp); fp_l.append(fp)
            ks_l.append(ks); vs_l.append(vs); fs_l.append(fs)
            s5rp_l.append(hrp); s5ip_l.append(hip); s5rs_l.append(hrs); s5is_l.append(his)
        else:
            cw = (conv_w_in[j], conv_b_in[j], conv_dw[j], conv_dw_b[j],
                  conv_ln_g[j], conv_ln_b[j], conv_w_out[j])
            zeros_c = jnp.zeros((bp, CONV_WIDTH - 1, D_MODEL), hp.dtype)
            op, cvp = conv_module(hp, zeros_c, *cw)
            os_, cvs = conv_module(hs, state_conv[j], *cw)
            cvp_l.append(cvp); cvs_l.append(cvs)
        xp = xp + gtp1 * rmsnorm(op, norm_mix_post[i])
        xs = xs + gts1 * rmsnorm(os_, norm_mix_post[i])
        hp = modulate(rmsnorm(xp, norm_ffn_pre[i]), shp2, scp2)
        hs = modulate(rmsnorm(xs, norm_ffn_pre[i]), shs2, scs2)
        xp = xp + gtp2 * rmsnorm(swiglu(hp, ffn_w_gate[i], ffn_w_up[i], ffn_w_down[i]), norm_ffn_post[i])
        xs = xs + gts2 * rmsnorm(swiglu(hs, ffn_w_gate[i], ffn_w_up[i], ffn_w_down[i]), norm_ffn_post[i])
    return (xp, xs,
            jnp.stack(kp_l), jnp.stack(vp_l), jnp.stack(fp_l),
            jnp.stack(ks_l), jnp.stack(vs_l), jnp.stack(fs_l),
            jnp.stack(s5rp_l), jnp.stack(s5ip_l), jnp.stack(s5rs_l), jnp.stack(s5is_l),
            jnp.stack(cvp_l), jnp.stack(cvs_l))
```

```python
import functools
import math

import jax
import jax.numpy as jnp
from jax import lax
from jax.experimental import pallas as pl
from jax.experimental.pallas import tpu as pltpu

_BF16 = jnp.bfloat16
_F32 = jnp.float32
_HI = lax.Precision.HIGHEST

RMS_EPS = 1e-6
LN_EPS = 1e-5

_V7X_VMEM_BYTES = 64 * 1024 * 1024
_LANES = 128
_SUBLANES = 8
_VMEM_LIMIT = _V7X_VMEM_BYTES - 8 * 1024 * 1024
_S5_CHUNK = 16
_NT = (((1,), (1,)), ((), ()))


def _params(*sem):
    return pltpu.CompilerParams(dimension_semantics=sem, vmem_limit_bytes=_VMEM_LIMIT)


def _tile(dim, pref, align):
    if dim <= pref:
        return dim
    t = (pref // align) * align
    while t >= align:
        if dim % t == 0:
            return t
        t -= align
    return dim


def _mm_body(*refs, n_w, n_b, epilogue):
    x_ref = refs[0]
    w_refs = refs[1:1 + n_w]
    b_refs = refs[1 + n_w:1 + n_w + n_b]
    o_ref = refs[-1]
    x = x_ref[...]
    accs = [jnp.dot(x, w[...], preferred_element_type=_F32) for w in w_refs]
    if n_b:
        accs = [a + b[...] for a, b in zip(accs, b_refs)]
    o_ref[...] = epilogue(*accs).astype(o_ref.dtype)


def _mm(x, ws, bs, epilogue, out_dtype, tm, tn, name):
    m, k = x.shape
    n = ws[0].shape[1]
    tm = _tile(m, tm, 2 * _SUBLANES)
    tn = _tile(n, tn, _LANES)
    in_specs = ([pl.BlockSpec((tm, k), lambda i, j: (i, 0))]
                + [pl.BlockSpec((k, tn), lambda i, j: (0, j))] * len(ws)
                + [pl.BlockSpec((1, tn), lambda i, j: (0, j))] * len(bs))
    return pl.pallas_call(
        functools.partial(_mm_body, n_w=len(ws), n_b=len(bs), epilogue=epilogue),
        grid=(m // tm, n // tn),
        in_specs=in_specs,
        out_specs=pl.BlockSpec((tm, tn), lambda i, j: (i, j)),
        out_shape=jax.ShapeDtypeStruct((m, n), out_dtype),
        compiler_params=_params("parallel", "arbitrary"),
        name=name,
    )(x, *ws, *bs)


def _ep_id(a):
    return a


def _ep_swiglu(g, u):
    return g * jax.nn.sigmoid(g) * u


def _ep_glu(a, b):
    return a * jax.nn.sigmoid(b)


def _ep_logsigmoid(a):
    return jnp.minimum(a, 0.0) - jnp.log(1.0 + jnp.exp(-jnp.abs(a)))


def _mod_body(c_ref, w_ref, b_ref, o_ref):
    c = c_ref[...]
    a = (c * jax.nn.sigmoid(c)).astype(_BF16)
    o_ref[...] = jnp.dot(a, w_ref[...].astype(_BF16), preferred_element_type=_F32) + b_ref[...]


def _modulation(c, mod_w, mod_b):
    depth, d, n = mod_w.shape
    rows = c.shape[0]
    tn = _tile(n, 512, _LANES)
    return pl.pallas_call(
        _mod_body,
        grid=(depth, n // tn),
        in_specs=[pl.BlockSpec((rows, d), lambda l, j: (0, 0)),
                  pl.BlockSpec((None, d, tn), lambda l, j: (l, 0, j)),
                  pl.BlockSpec((None, 1, tn), lambda l, j: (l, 0, j))],
        out_specs=pl.BlockSpec((None, rows, tn), lambda l, j: (l, 0, j)),
        out_shape=jax.ShapeDtypeStruct((depth, rows, n), _F32),
        compiler_params=_params("parallel", "arbitrary"),
        name="modulation",
    )(c, mod_w, mod_b.reshape(depth, 1, n))


def _rms(x, g):
    return x * lax.rsqrt(jnp.mean(x * x, axis=-1, keepdims=True) + RMS_EPS) * g


def _norm_mod_body(x_ref, g_ref, sc_ref, sh_ref, h_ref):
    y = _rms(x_ref[...], g_ref[...])
    h_ref[...] = (y * (1.0 + sc_ref[...]) + sh_ref[...]).astype(h_ref.dtype)


def _res_norm_body(x_ref, o_ref, gpost_ref, gate_ref, *rest, with_next):
    xn = x_ref[...] + gate_ref[...] * _rms(o_ref[...].astype(_F32), gpost_ref[...])
    if with_next:
        gpre_ref, sc_ref, sh_ref, xn_ref, h_ref = rest
        xn_ref[...] = xn
        h_ref[...] = (_rms(xn, gpre_ref[...]) * (1.0 + sc_ref[...]) + sh_ref[...]).astype(h_ref.dtype)
    else:
        (xn_ref,) = rest
        xn_ref[...] = xn


def _row_specs(rows, d, tr, seq):
    per = seq // tr
    row = pl.BlockSpec((tr, d), lambda i: (i, 0))
    vec = pl.BlockSpec((1, d), lambda i: (0, 0))
    bvec = pl.BlockSpec((None, 1, d), lambda i: (i // per, 0, 0))
    return row, vec, bvec


def _norm_mod(x, g, scale, shift, seq):
    rows, d = x.shape
    tr = _tile(seq, 256, _SUBLANES)
    row, vec, bvec = _row_specs(rows, d, tr, seq)
    return pl.pallas_call(
        _norm_mod_body,
        grid=(rows // tr,),
        in_specs=[row, vec, bvec, bvec],
        out_specs=row,
        out_shape=jax.ShapeDtypeStruct((rows, d), _BF16),
        compiler_params=_params("parallel"),
        name="norm_mod",
    )(x, g.reshape(1, d), scale, shift)


def _res_norm(x, o, gpost, gate, nxt, seq):
    rows, d = x.shape
    tr = _tile(seq, 256, _SUBLANES)
    row, vec, bvec = _row_specs(rows, d, tr, seq)
    xs = jax.ShapeDtypeStruct((rows, d), _F32)
    if nxt is None:
        return pl.pallas_call(
            functools.partial(_res_norm_body, with_next=False),
            grid=(rows // tr,),
            in_specs=[row, row, vec, bvec],
            out_specs=row,
            out_shape=xs,
            compiler_params=_params("parallel"),
            name="res_norm_last",
        )(x, o, gpost.reshape(1, d), gate), None
    g, scale, shift = nxt
    return pl.pallas_call(
        functools.partial(_res_norm_body, with_next=True),
        grid=(rows // tr,),
        in_specs=[row, row, vec, bvec, vec, bvec, bvec],
        out_specs=(row, row),
        out_shape=(xs, jax.ShapeDtypeStruct((rows, d), _BF16)),
        compiler_params=_params("parallel"),
        name="res_norm",
    )(x, o, gpost.reshape(1, d), gate, g.reshape(1, d), scale, shift)


def _cumsum_body(x_ref, o_ref, *, blk):
    seq = x_ref.shape[0]
    r = lax.broadcasted_iota(jnp.int32, (blk, blk), 0)
    c = lax.broadcasted_iota(jnp.int32, (blk, blk), 1)
    tri = (c <= r).astype(_F32)
    carry = jnp.zeros((1, x_ref.shape[1]), _F32)
    for i in range(seq // blk):
        y = jnp.dot(tri, x_ref[i * blk:(i + 1) * blk, :], precision=_HI,
                    preferred_element_type=_F32) + carry
        o_ref[i * blk:(i + 1) * blk, :] = y
        carry = y[blk - 1:blk, :]


def _cumsum_rows(x, seq):
    rows, n = x.shape
    blk = _tile(seq, 256, _SUBLANES)
    return pl.pallas_call(
        functools.partial(_cumsum_body, blk=blk),
        grid=(rows // seq,),
        in_specs=[pl.BlockSpec((seq, n), lambda b: (b, 0))],
        out_specs=pl.BlockSpec((seq, n), lambda b: (b, 0)),
        out_shape=jax.ShapeDtypeStruct((rows, n), _F32),
        compiler_params=_params("parallel"),
        name="logf_cumsum",
    )(x)


def _fox_prompt_body(q_ref, k_ref, v_ref, cq_ref, ck_ref, o_ref, kb_ref, vb_ref, *, tq, scale):
    qi = pl.program_id(2)

    @pl.when(qi == 0)
    def _():
        kb_ref[...] = k_ref[...].astype(_BF16)
        vb_ref[...] = v_ref[...].astype(_BF16)

    dh = q_ref.shape[1]
    q = (q_ref[...] * scale).astype(_BF16)
    cq = cq_ref[...]

    def step(kj, carry, masked):
        m, l, acc = carry
        k0 = pl.multiple_of(kj * tq, tq)
        k = kb_ref[pl.ds(k0, tq), :]
        v = vb_ref[pl.ds(k0, tq), :]
        s = lax.dot_general(q, k, _NT, preferred_element_type=_F32)
        s = s + (cq - ck_ref[:, pl.ds(k0, tq)])
        if masked:
            row = lax.broadcasted_iota(jnp.int32, (tq, tq), 0)
            col = lax.broadcasted_iota(jnp.int32, (tq, tq), 1)
            s = jnp.where(col <= row, s, -jnp.inf)
        m_new = jnp.maximum(m, jnp.max(s, axis=1, keepdims=True))
        alpha = jnp.exp(m - m_new)
        p = jnp.exp(s - m_new)
        l = alpha * l + jnp.sum(p, axis=1, keepdims=True)
        acc = alpha * acc + jnp.dot(p.astype(_BF16), v, preferred_element_type=_F32)
        return m_new, l, acc

    init = (jnp.full((tq, 1), -jnp.inf, _F32), jnp.zeros((tq, 1), _F32), jnp.zeros((tq, dh), _F32))
    carry = lax.fori_loop(0, qi, lambda kj, c: step(kj, c, False), init)
    _, l, acc = step(qi, carry, True)
    o_ref[...] = (acc / l).astype(o_ref.dtype)


def _fox_prompt(z, cum_col, cum_row, bsz, seq, heads, dh):
    tq = _tile(seq, 512, _LANES)
    nq = seq // tq
    return pl.pallas_call(
        functools.partial(_fox_prompt_body, tq=tq, scale=dh ** -0.5),
        grid=(bsz, heads, nq),
        in_specs=[pl.BlockSpec((tq, dh), lambda b, h, i: (b * nq + i, h)),
                  pl.BlockSpec((seq, dh), lambda b, h, i: (b, heads + h)),
                  pl.BlockSpec((seq, dh), lambda b, h, i: (b, 2 * heads + h)),
                  pl.BlockSpec((None, None, tq, 1), lambda b, h, i: (b, h, i, 0)),
                  pl.BlockSpec((None, None, 1, seq), lambda b, h, i: (b, h, 0, 0))],
        out_specs=pl.BlockSpec((tq, dh), lambda b, h, i: (b * nq + i, h)),
        out_shape=jax.ShapeDtypeStruct((bsz * seq, heads * dh), _BF16),
        scratch_shapes=[pltpu.VMEM((seq, dh), _BF16), pltpu.VMEM((seq, dh), _BF16)],
        compiler_params=_params("parallel", "parallel", "arbitrary"),
        name="fox_prompt",
    )(z, z, z, cum_col, cum_row)


def _fox_sample_body(pt_ref, qbd_ref, kn_ref, vn_ref, gn_ref, kc_ref, vc_ref, gc_ref, o_ref,
                     m_ref, l_ref, acc_ref, carry_ref, lq_ref, *, heads, nq, dh):
    del pt_ref
    step = pl.program_id(1)
    hq = heads * nq
    page = kn_ref.shape[0]
    expand = (lax.broadcasted_iota(jnp.int32, (hq, heads), 0) // nq
              == lax.broadcasted_iota(jnp.int32, (hq, heads), 1)).astype(_F32)
    jj = lax.broadcasted_iota(jnp.int32, (page, page), 0)
    kk = lax.broadcasted_iota(jnp.int32, (page, page), 1)

    def attend(kpage, vpage, bias, valid):
        s = lax.dot_general(qbd_ref[...], kpage.astype(_BF16), _NT, preferred_element_type=_F32) + bias
        if valid is not None:
            s = jnp.where(valid, s, -jnp.inf)
        m_prev = m_ref[...]
        m_new = jnp.maximum(m_prev, jnp.max(s, axis=1, keepdims=True))
        alpha = jnp.exp(m_prev - m_new)
        p = jnp.exp(s - m_new)
        l_ref[...] = alpha * l_ref[...] + jnp.sum(p, axis=1, keepdims=True)
        acc_ref[...] = alpha * acc_ref[...] + jnp.dot(p.astype(_BF16), vpage.astype(_BF16),
                                                      preferred_element_type=_F32)
        m_ref[...] = m_new

    @pl.when(step == 0)
    def _():
        gexp = lax.dot_general(expand, gn_ref[...], _NT, precision=_HI, preferred_element_type=_F32)
        lkeys = jnp.dot(gexp, (jj <= kk).astype(_F32), precision=_HI, preferred_element_type=_F32)
        qrow = lax.broadcasted_iota(jnp.int32, (hq, page), 0) % nq
        lane = lax.broadcasted_iota(jnp.int32, (hq, page), 1)
        lq = jnp.sum(jnp.where(lane == qrow, lkeys, 0.0), axis=1, keepdims=True)
        lq_ref[...] = lq
        carry_ref[...] = jnp.zeros_like(carry_ref)
        m_ref[...] = jnp.full_like(m_ref, -jnp.inf)
        l_ref[...] = jnp.zeros_like(l_ref)
        acc_ref[...] = jnp.zeros_like(acc_ref)
        attend(kn_ref[...], vn_ref[...], lq - lkeys, lane <= qrow)

    @pl.when(step > 0)
    def _():
        gexp = lax.dot_general(expand, gc_ref[...], _NT, precision=_HI, preferred_element_type=_F32)
        suffix = jnp.dot(gexp, (jj > kk).astype(_F32), precision=_HI, preferred_element_type=_F32)
        carry = carry_ref[...]
        attend(kc_ref[...], vc_ref[...], lq_ref[...] + carry + suffix, None)
        carry_ref[...] = carry + jnp.sum(gexp, axis=1, keepdims=True)

    @pl.when(step == pl.num_programs(1) - 1)
    def _():
        inv = 1.0 / l_ref[...]
        for h in range(heads):
            rows = slice(h * nq, (h + 1) * nq)
            cols = slice(h * dh, (h + 1) * dh)
            o_ref[:, cols] = (acc_ref[rows, cols] * inv[rows]).astype(o_ref.dtype)


def _fox_sample(qbd, knew, vnew, gnew, cache_k, cache_v, cache_logf, page_table, layer, nq, heads, dh):
    bsz, n_pages = page_table.shape
    ne, n_pool, page = cache_k.shape[:3]
    width = heads * dh
    hq = heads * nq
    ck = cache_k.reshape(ne, n_pool, page, width)
    cv = cache_v.reshape(ne, n_pool, page, width)

    def new_map(b, s, pt):
        return (b, 0, 0)

    def page_map(b, s, pt):
        return (layer, pt[b * n_pages + n_pages - jnp.maximum(s, 1)], 0, 0)

    grid_spec = pltpu.PrefetchScalarGridSpec(
        num_scalar_prefetch=1,
        grid=(bsz, n_pages + 1),
        in_specs=[pl.BlockSpec((None, hq, width), new_map),
                  pl.BlockSpec((None, page, width), new_map),
                  pl.BlockSpec((None, page, width), new_map),
                  pl.BlockSpec((None, page, heads), new_map),
                  pl.BlockSpec((None, None, page, width), page_map),
                  pl.BlockSpec((None, None, page, width), page_map),
                  pl.BlockSpec((None, None, page, heads), page_map)],
        out_specs=pl.BlockSpec((None, nq, width), new_map),
        scratch_shapes=[pltpu.VMEM((hq, 1), _F32), pltpu.VMEM((hq, 1), _F32),
                        pltpu.VMEM((hq, width), _F32), pltpu.VMEM((hq, 1), _F32),
                        pltpu.VMEM((hq, 1), _F32)],
    )
    return pl.pallas_call(
        functools.partial(_fox_sample_body, heads=heads, nq=nq, dh=dh),
        grid_spec=grid_spec,
        out_shape=jax.ShapeDtypeStruct((bsz, nq, width), _BF16),
        compiler_params=_params("parallel", "arbitrary"),
        name="fox_sample",
    )(page_table.reshape(-1), qbd, knew, vnew, gnew, ck, cv, cache_logf)


def _gelu_tanh(y):
    return 0.5 * y * (1.0 + jnp.tanh(math.sqrt(2.0 / math.pi) * (y + 0.044715 * (y * y * y))))


def _s5_body(u_ref, wys_ref, wh_ref, wg_ref, dv_ref, gb_ref, a1_ref, a2_ref, h0_ref, y_ref, hf_ref,
             *, gpb, bsz, nc, tc, ns):
    rows = bsz * nc

    def one(g, _):
        u = u_ref[g]
        ub = u.astype(_BF16)
        wys = wys_ref[g]
        yin = jnp.dot(ub, wys[:, :tc].astype(_BF16), preferred_element_type=_F32)
        a1 = a1_ref[g]
        a2 = a2_ref[g]
        if nc == 1:
            st = jnp.dot(u, wys[:, tc:], precision=_HI, preferred_element_type=_F32)
            hstart = h0_ref[g]
            hf_ref[g] = a1[0:1] * hstart + a2[0:1] * pltpu.roll(hstart, ns, axis=1) + st
        else:
            x = jnp.dot(ub, wys[:, tc:].astype(_BF16), preferred_element_type=_F32)
            chunk = lax.broadcasted_iota(jnp.int32, (rows, 2 * ns), 0) % nc
            for k in range(nc.bit_length() - 1):
                sh = 1 << k
                xs = pltpu.roll(x, sh, axis=0)
                upd = a1[k:k + 1] * xs + a2[k:k + 1] * pltpu.roll(xs, ns, axis=1)
                x = x + jnp.where(chunk >= sh, upd, 0.0)
            hstart = jnp.where(chunk >= 1, pltpu.roll(x, 1, axis=0), 0.0)
            for b in range(bsz):
                hf_ref[g, b:b + 1, :] = x[(b + 1) * nc - 1:(b + 1) * nc, :]
        y = yin + jnp.dot(hstart.astype(_BF16), wh_ref[g].astype(_BF16), preferred_element_type=_F32)
        y = _gelu_tanh(y + dv_ref[g] * u)
        gate = jax.nn.sigmoid(jnp.dot(y.astype(_BF16), wg_ref[g].astype(_BF16),
                                      preferred_element_type=_F32) + gb_ref[g])
        y_ref[g] = (y * gate).astype(y_ref.dtype)
        return 0

    lax.fori_loop(0, gpb, one, 0)


def _s5_operators(lam_re, lam_im, log_step, b_re, b_im, c_re, c_im, d, glu_w, glu_b, t_len, n_scan):
    hp = dict(precision=_HI)
    groups, ns = lam_re.shape
    ch = b_re.shape[-1]
    dt = jnp.exp(log_step)[:, None]

    def power(tau):
        tau = tau.astype(_F32)[None, :, None]
        mag = jnp.exp(lam_re[:, None, :] * dt[:, None, :] * tau)
        ang = lam_im[:, None, :] * dt[:, None, :] * tau
        return mag * jnp.cos(ang), mag * jnp.sin(ang)

    ab_re, ab_im = (p[:, 0] for p in power(jnp.ones((1,))))
    nr, ni = ab_re - 1.0, ab_im
    den = lam_re * lam_re + lam_im * lam_im
    f_re = (nr * lam_re + ni * lam_im) / den
    f_im = (ni * lam_re - nr * lam_im) / den
    bb_re = f_re[..., None] * b_re - f_im[..., None] * b_im
    bb_im = f_re[..., None] * b_im + f_im[..., None] * b_re

    pr, pi = power(jnp.arange(t_len + 1))
    m_re = jnp.einsum('gtp,gop->gtop', pr, c_re, **hp) - jnp.einsum('gtp,gop->gtop', pi, c_im, **hp)
    m_im = jnp.einsum('gtp,gop->gtop', pi, c_re, **hp) + jnp.einsum('gtp,gop->gtop', pr, c_im, **hp)
    lag = (jnp.einsum('gtop,gpi->gtio', m_re[:, :t_len], bb_re, **hp)
           - jnp.einsum('gtop,gpi->gtio', m_im[:, :t_len], bb_im, **hp))
    s_idx = jnp.arange(t_len)[:, None]
    t_idx = jnp.arange(t_len)[None, :]
    toe = lag[:, jnp.maximum(t_idx - s_idx, 0)]
    toe = jnp.where((t_idx >= s_idx)[None, :, :, None, None], toe, 0.0)
    wy = toe.transpose(0, 1, 3, 2, 4).reshape(groups, t_len * ch, t_len * ch)
    er, ei = pr[:, t_len - 1 - jnp.arange(t_len)], pi[:, t_len - 1 - jnp.arange(t_len)]
    ws_re = er[..., None] * bb_re[:, None] - ei[..., None] * bb_im[:, None]
    ws_im = er[..., None] * bb_im[:, None] + ei[..., None] * bb_re[:, None]
    ws = jnp.concatenate([ws_re.transpose(0, 1, 3, 2), ws_im.transpose(0, 1, 3, 2)], axis=-1)
    ws = ws.reshape(groups, t_len * ch, 2 * ns)
    wys = jnp.concatenate([wy, ws], axis=-1)
    wh = jnp.concatenate([m_re[:, 1:].transpose(0, 3, 1, 2), -m_im[:, 1:].transpose(0, 3, 1, 2)], axis=1)
    wh = wh.reshape(groups, 2 * ns, t_len * ch)
    eye = jnp.eye(t_len, dtype=_F32)
    wg = (eye[None, :, None, :, None] * glu_w[:, None, :, None, :]).reshape(groups, t_len * ch, t_len * ch)
    dv = jnp.tile(d, (1, t_len)).reshape(groups, 1, t_len * ch)
    gb = jnp.tile(glu_b, (1, t_len)).reshape(groups, 1, t_len * ch)
    sr, si = power(t_len * (2 ** jnp.arange(_SUBLANES)))
    keep = (jnp.arange(_SUBLANES) < max(n_scan, 1))[None, :, None]
    sr, si = jnp.where(keep, sr, 0.0), jnp.where(keep, si, 0.0)
    a1 = jnp.concatenate([sr, sr], axis=-1)
    a2 = jnp.concatenate([-si, si], axis=-1)
    return wys, wh, wg, dv, gb, a1, a2


def _s5(u, h0, ops, bsz, seq, t_len):
    wys, wh, wg, dv, gb, a1, a2 = ops
    groups, tc, _ = wg.shape
    ch = tc // t_len
    ns2 = wh.shape[1]
    nc = seq // t_len
    rows = bsz * nc
    ug = u.reshape(bsz, nc, t_len, groups, ch).transpose(3, 0, 1, 2, 4).reshape(groups, rows, tc)
    if h0 is None:
        h0 = jnp.zeros((groups, bsz, ns2), _F32)
    gpb = _tile(groups, 8, 1)

    def gmap(i):
        return (i, 0, 0)

    y, hf = pl.pallas_call(
        functools.partial(_s5_body, gpb=gpb, bsz=bsz, nc=nc, tc=tc, ns=ns2 // 2),
        grid=(groups // gpb,),
        in_specs=[pl.BlockSpec((gpb, rows, tc), gmap),
                  pl.BlockSpec((gpb, tc, tc + ns2), gmap),
                  pl.BlockSpec((gpb, ns2, tc), gmap),
                  pl.BlockSpec((gpb, tc, tc), gmap),
                  pl.BlockSpec((gpb, 1, tc), gmap),
                  pl.BlockSpec((gpb, 1, tc), gmap),
                  pl.BlockSpec((gpb, _SUBLANES, ns2), gmap),
                  pl.BlockSpec((gpb, _SUBLANES, ns2), gmap),
                  pl.BlockSpec((gpb, bsz, ns2), gmap)],
        out_specs=(pl.BlockSpec((gpb, rows, tc), gmap), pl.BlockSpec((gpb, bsz, ns2), gmap)),
        out_shape=(jax.ShapeDtypeStruct((groups, rows, tc), _BF16),
                   jax.ShapeDtypeStruct((groups, bsz, ns2), _F32)),
        compiler_params=_params("parallel"),
        name="s5_chunks",
    )(ug, wys, wh, wg, dv, gb, a1, a2, h0)
    y = y.reshape(groups, bsz, nc, t_len, ch).transpose(1, 2, 3, 0, 4).reshape(bsz * seq, groups * ch)
    return y, hf


def _conv_body(u_ref, halo_ref, st_ref, w_ref, wb_ref, lg_ref, lb_ref, o_ref, win_ref, y_ref,
               *, nblk, taps, rc, cc):
    i = pl.program_id(0)
    rows, d = u_ref.shape
    hb = halo_ref.shape[0]
    first = (i % nblk) == 0

    @pl.when(first)
    def _():
        win_ref[0:hb, :] = st_ref[...]

    @pl.when(jnp.logical_not(first))
    def _():
        win_ref[0:hb, :] = halo_ref[...]

    win_ref[hb:hb + rows, :] = u_ref[...]
    off = hb - (taps - 1)

    def col_loop(ci, _):
        c0 = pl.multiple_of(ci * cc, cc)
        bias = wb_ref[:, pl.ds(c0, cc)]
        for r0 in range(0, rows, rc):
            acc = jnp.zeros((rc, cc), _F32) + bias
            for j in range(taps):
                acc = acc + win_ref[pl.ds(r0 + off + j, rc), pl.ds(c0, cc)] * w_ref[pl.ds(j, 1), pl.ds(c0, cc)]
            y_ref[r0:r0 + rc, pl.ds(c0, cc)] = acc
        return 0

    lax.fori_loop(0, d // cc, col_loop, 0)
    y = y_ref[...]
    mu = jnp.mean(y, axis=-1, keepdims=True)
    yc = y - mu
    var = jnp.mean(yc * yc, axis=-1, keepdims=True)
    z = yc * lax.rsqrt(var + LN_EPS) * lg_ref[...] + lb_ref[...]
    o_ref[...] = (z * jax.nn.sigmoid(z)).astype(o_ref.dtype)


def _conv_ln_silu(u, state, dw, dw_b, ln_g, ln_b, seq):
    rows_total, d = u.shape
    taps = dw.shape[0]
    bsz = rows_total // seq
    hb = -(-(taps - 1) // _SUBLANES) * _SUBLANES
    st = jnp.pad(state.astype(_F32), ((0, 0), (hb - (taps - 1), 0), (0, 0)))
    rows = _tile(seq, 256, hb) if seq % hb == 0 else seq
    nblk = seq // rows
    rc = _tile(rows, 64, _SUBLANES)
    cc = _tile(d, 256, _LANES)
    per = rows // hb if nblk > 1 else 1
    halo_src = u if nblk > 1 else st[0]

    def halo_map(i):
        return (jnp.maximum(i * per - 1, 0) if nblk > 1 else 0, 0)

    vec = pl.BlockSpec((1, d), lambda i: (0, 0))
    return pl.pallas_call(
        functools.partial(_conv_body, nblk=nblk, taps=taps, rc=rc, cc=cc),
        grid=(rows_total // rows,),
        in_specs=[pl.BlockSpec((rows, d), lambda i: (i, 0)),
                  pl.BlockSpec((hb, d), halo_map),
                  pl.BlockSpec((None, hb, d), lambda i: (i // nblk, 0, 0)),
                  pl.BlockSpec((taps, d), lambda i: (0, 0)),
                  vec, vec, vec],
        out_specs=pl.BlockSpec((rows, d), lambda i: (i, 0)),
        out_shape=jax.ShapeDtypeStruct((rows_total, d), _BF16),
        scratch_shapes=[pltpu.VMEM((hb + rows, d), _F32), pltpu.VMEM((rows, d), _F32)],
        compiler_params=_params("parallel"),
        name="conv_ln_silu",
    )(u, halo_src, st, dw, dw_b.reshape(1, d), ln_g.reshape(1, d), ln_b.reshape(1, d))


def kernel(x_prompt, x_sample, cache_k, cache_v, cache_logf, state_s5_re, state_s5_im, state_conv, page_table, c_prompt, c_sample, mod_w, mod_b, norm_mix_pre, norm_mix_post, norm_ffn_pre, norm_ffn_post, ffn_w_gate, ffn_w_up, ffn_w_down, hyb_w_in, hyb_b_f, hyb_w_out, s5_lambda_re, s5_lambda_im, s5_log_step, s5_b_re, s5_b_im, s5_c_re, s5_c_im, s5_d, s5_glu_w, s5_glu_b, conv_w_in, conv_b_in, conv_dw, conv_dw_b, conv_ln_g, conv_ln_b, conv_w_out):
    bp, lp, d = x_prompt.shape
    bs, ls, _ = x_sample.shape
    depth = mod_w.shape[0]
    heads, dh = cache_k.shape[3], cache_k.shape[4]
    fw = heads * dh
    groups, ns = s5_lambda_re.shape[1], s5_lambda_re.shape[2]
    ch = s5_b_re.shape[-1]
    taps = conv_dw.shape[1]
    page = cache_k.shape[2]
    seqs = ((bp, lp), (bs, ls))

    n_c = bp + bs
    c_rows = -(-n_c // (2 * _SUBLANES)) * (2 * _SUBLANES)
    c_all = jnp.pad(jnp.concatenate([c_prompt, c_sample], axis=0), ((0, c_rows - n_c), (0, 0)))
    mod = _modulation(c_all, mod_w, mod_b).reshape(depth, c_rows, 6, d)

    def mod_terms(i, grp):
        lo, hi = (0, bp) if grp == 0 else (bp, bp + bs)
        return [mod[i, lo:hi, t][:, None, :] for t in range(6)]

    xs = [x_prompt.reshape(bp * lp, d), x_sample.reshape(bs * ls, d)]
    terms = [mod_terms(0, g) for g in range(2)]
    hs = [_norm_mod(xs[g], norm_mix_pre[0], terms[g][1], terms[g][0], seqs[g][1]) for g in range(2)]

    outs = {k: ([], []) for k in ("k", "v", "f", "s5r", "s5i", "conv")}

    for i in range(depth):
        j = i // 2
        if i % 2 == 0:
            w_in = hyb_w_in[j]
            w_main = jnp.concatenate([w_in[:, :3 * fw], w_in[:, 3 * fw + heads:]], axis=1).astype(_BF16)
            w_f = jnp.pad(w_in[:, 3 * fw:3 * fw + heads], ((0, 0), (0, _LANES - heads))).astype(_BF16)
            b_f = jnp.pad(hyb_b_f[j], (0, _LANES - heads)).reshape(1, _LANES)
            w_out = hyb_w_out[j].astype(_BF16)
            s5_raw = (s5_lambda_re[j], s5_lambda_im[j], s5_log_step[j], s5_b_re[j], s5_b_im[j],
                      s5_c_re[j], s5_c_im[j], s5_d[j], s5_glu_w[j], s5_glu_b[j])
            mixed = []
            for g, (bsz, seq) in enumerate(seqs):
                z = _mm(hs[g], [w_main], [], _ep_id, _F32, 1024, 512, "in_proj")
                logf = _mm(hs[g], [w_f], [b_f], _ep_logsigmoid, _F32, 1024, _LANES, "forget_gate")
                outs["k"][g].append(z[:, fw:2 * fw].reshape(bsz, seq, heads, dh))
                outs["v"][g].append(z[:, 2 * fw:3 * fw].reshape(bsz, seq, heads, dh))
                outs["f"][g].append(logf[:, :heads].reshape(bsz, seq, heads))
                if g == 0:
                    cum = _cumsum_rows(logf, seq)[:, :heads].reshape(bsz, seq, heads).transpose(0, 2, 1)
                    att = _fox_prompt(z, cum[..., None], cum[:, :, None, :], bsz, seq, heads, dh)
                    t_len = _S5_CHUNK if seq % _S5_CHUNK == 0 else seq
                    h0 = None
                else:
                    q = z[:, :fw].reshape(bsz, seq, heads, dh) * (dh ** -0.5)
                    eye = jnp.eye(heads, dtype=_F32)
                    qbd = (q.transpose(0, 2, 1, 3)[:, :, :, None, :] * eye[None, :, None, :, None])
                    qbd = qbd.reshape(bsz, heads * seq, fw).astype(_BF16)
                    pad = ((0, 0), (0, page - seq), (0, 0))
                    knew = jnp.pad(z[:, fw:2 * fw].reshape(bsz, seq, fw), pad)
                    vnew = jnp.pad(z[:, 2 * fw:3 * fw].reshape(bsz, seq, fw), pad)
                    gnew = jnp.pad(logf[:, :heads].reshape(bsz, seq, heads), pad)
                    att = _fox_sample(qbd, knew, vnew, gnew, cache_k, cache_v, cache_logf, page_table,
                                      j, seq, heads, dh).reshape(bsz * seq, fw)
                    t_len = seq
                    h0 = jnp.concatenate([state_s5_re[j], state_s5_im[j]], axis=-1).transpose(1, 0, 2)
                n_scan = (seq // t_len).bit_length() - 1
                ops = _s5_operators(*s5_raw, t_len, n_scan)
                ssm, hf = _s5(z[:, 3 * fw:], h0, ops, bsz, seq, t_len)
                hf = hf.transpose(1, 0, 2)
                outs["s5r"][g].append(hf[..., :ns])
                outs["s5i"][g].append(hf[..., ns:])
                mix_in = jnp.concatenate([att, ssm], axis=-1)
                mixed.append(_mm(mix_in, [w_out], [], _ep_id, _F32, 1024, 512, "out_proj"))
        else:
            wc = conv_w_in[j]
            wa, wb = wc[:, :d].astype(_BF16), wc[:, d:].astype(_BF16)
            ba, bb = conv_b_in[j][:d].reshape(1, d), conv_b_in[j][d:].reshape(1, d)
            w_out = conv_w_out[j].astype(_BF16)
            mixed = []
            for g, (bsz, seq) in enumerate(seqs):
                u = _mm(hs[g], [wa, wb], [ba, bb], _ep_glu, _F32, 1024, 256, "conv_in_glu")
                state = jnp.zeros((bsz, taps - 1, d), _F32) if g == 0 else state_conv[j]
                up = jnp.concatenate([state, u.reshape(bsz, seq, d)], axis=1)
                outs["conv"][g].append(up[:, -(taps - 1):])
                y = _conv_ln_silu(u, state, conv_dw[j], conv_dw_b[j], conv_ln_g[j], conv_ln_b[j], seq)
                mixed.append(_mm(y, [w_out], [], _ep_id, _F32, 1024, 512, "conv_out"))

        wg, wu, wd = ffn_w_gate[i].astype(_BF16), ffn_w_up[i].astype(_BF16), ffn_w_down[i].astype(_BF16)
        nxt_terms = [mod_terms(i + 1, g) for g in range(2)] if i + 1 < depth else None
        for g, (bsz, seq) in enumerate(seqs):
            t = terms[g]
            x1, h2 = _res_norm(xs[g], mixed[g], norm_mix_post[i], t[2], (norm_ffn_pre[i], t[4], t[3]), seq)
            a = _mm(h2, [wg, wu], [], _ep_swiglu, _BF16, 1024, 256, "ffn_gate_up")
            f = _mm(a, [wd], [], _ep_id, _F32, 512, 256, "ffn_down")
            nxt = None if nxt_terms is None else (norm_mix_pre[i + 1], nxt_terms[g][1], nxt_terms[g][0])
            xs[g], hs[g] = _res_norm(x1, f, norm_ffn_post[i], t[5], nxt, seq)
        terms = nxt_terms

    def stack(key, g):
        return jnp.stack(outs[key][g])

    return (xs[0].reshape(bp, lp, d), xs[1].reshape(bs, ls, d),
            stack("k", 0), stack("v", 0), stack("f", 0),
            stack("k", 1), stack("v", 1), stack("f", 1),
            stack("s5r", 0), stack("s5i", 0), stack("s5r", 1), stack("s5i", 1),
            stack("conv", 0), stack("conv", 1))
```

```python
import functools
import math

import jax
import jax.numpy as jnp
from jax import lax
from jax.experimental import pallas as pl
from jax.experimental.pallas import tpu as pltpu

_BF16 = jnp.bfloat16
_F32 = jnp.float32
_HI = lax.Precision.HIGHEST

RMS_EPS = 1e-6
LN_EPS = 1e-5

_V7X_VMEM_BYTES = 64 * 1024 * 1024
_LANES = 128
_SUBLANES = 8
_VMEM_LIMIT = _V7X_VMEM_BYTES - 8 * 1024 * 1024
_S5_CHUNK = 16
_NT = (((1,), (1,)), ((), ()))


def _params(*sem):
    return pltpu.CompilerParams(dimension_semantics=sem, vmem_limit_bytes=_VMEM_LIMIT)


def _tile(dim, pref, align):
    if dim <= pref:
        return dim
    t = (pref // align) * align
    while t >= align:
        if dim % t == 0:
            return t
        t -= align
    return dim


def _mm_body(*refs, n_w, n_b, epilogue):
    x_ref = refs[0]
    w_refs = refs[1:1 + n_w]
    b_refs = refs[1 + n_w:1 + n_w + n_b]
    o_ref = refs[-1]
    x = x_ref[...]
    accs = [jnp.dot(x, w[...].astype(_BF16), preferred_element_type=_F32) for w in w_refs]
    if n_b:
        accs = [a + b[...] for a, b in zip(accs, b_refs)]
    o_ref[...] = epilogue(*accs).astype(o_ref.dtype)


def _mm(x, ws, bs, n, epilogue, out_dtype, tm, tn, name, single_buffer_x=False):
    m, k = x.shape
    tm = _tile(m, tm, 2 * _SUBLANES)
    tn = _tile(n, tn, _LANES)

    def spec(rows, layer, col0):
        assert col0 % tn == 0
        return pl.BlockSpec((None, rows, tn), lambda i, j: (layer, 0, col0 // tn + j))

    x_mode = dict(pipeline_mode=pl.Buffered(1)) if single_buffer_x and m > tm else {}
    in_specs = ([pl.BlockSpec((tm, k), lambda i, j: (i, 0), **x_mode)]
                + [spec(k, layer, col0) for _, layer, col0 in ws]
                + [spec(1, layer, col0) for _, layer, col0 in bs])
    return pl.pallas_call(
        functools.partial(_mm_body, n_w=len(ws), n_b=len(bs), epilogue=epilogue),
        grid=(m // tm, n // tn),
        in_specs=in_specs,
        out_specs=pl.BlockSpec((tm, tn), lambda i, j: (i, j)),
        out_shape=jax.ShapeDtypeStruct((m, n), out_dtype),
        compiler_params=_params("parallel", "arbitrary"),
        name=name,
    )(x, *[w for w, _, _ in ws], *[b for b, _, _ in bs])


def _ep_id(a):
    return a


def _ep_swiglu(g, u):
    return g * jax.nn.sigmoid(g) * u


def _ep_glu(a, b):
    return a * jax.nn.sigmoid(b)


def _ep_logsigmoid(a):
    return jnp.minimum(a, 0.0) - jnp.log(1.0 + jnp.exp(-jnp.abs(a)))


def _mod_body(c_ref, w_ref, b_ref, o_ref):
    c = c_ref[...]
    a = (c * jax.nn.sigmoid(c)).astype(_BF16)
    o_ref[...] = jnp.dot(a, w_ref[...].astype(_BF16), preferred_element_type=_F32) + b_ref[...]


def _modulation(c, mod_w, mod_b):
    depth, d, n = mod_w.shape
    rows = c.shape[0]
    tn = _tile(n, 512, _LANES)
    return pl.pallas_call(
        _mod_body,
        grid=(depth, n // tn),
        in_specs=[pl.BlockSpec((rows, d), lambda l, j: (0, 0)),
                  pl.BlockSpec((None, d, tn), lambda l, j: (l, 0, j)),
                  pl.BlockSpec((None, 1, tn), lambda l, j: (l, 0, j))],
        out_specs=pl.BlockSpec((None, rows, tn), lambda l, j: (l, 0, j)),
        out_shape=jax.ShapeDtypeStruct((depth, rows, n), _F32),
        compiler_params=_params("parallel", "arbitrary"),
        name="modulation",
    )(c, mod_w, mod_b.reshape(depth, 1, n))


def _rms(x, g):
    return x * lax.rsqrt(jnp.mean(x * x, axis=-1, keepdims=True) + RMS_EPS) * g


def _norm_mod_body(x_ref, g_ref, sc_ref, sh_ref, h_ref):
    y = _rms(x_ref[...], g_ref[...])
    h_ref[...] = (y * (1.0 + sc_ref[...]) + sh_ref[...]).astype(h_ref.dtype)


def _res_norm_body(x_ref, o_ref, gpost_ref, gate_ref, *rest, with_next):
    xn = x_ref[...] + gate_ref[...] * _rms(o_ref[...].astype(_F32), gpost_ref[...])
    if with_next:
        gpre_ref, sc_ref, sh_ref, xn_ref, h_ref = rest
        xn_ref[...] = xn
        h_ref[...] = (_rms(xn, gpre_ref[...]) * (1.0 + sc_ref[...]) + sh_ref[...]).astype(h_ref.dtype)
    else:
        (xn_ref,) = rest
        xn_ref[...] = xn


def _row_specs(rows, d, tr, seq):
    per = seq // tr
    row = pl.BlockSpec((tr, d), lambda i: (i, 0))
    vec = pl.BlockSpec((1, d), lambda i: (0, 0))
    bvec = pl.BlockSpec((None, 1, d), lambda i: (i // per, 0, 0))
    return row, vec, bvec


def _norm_mod(x, g, scale, shift, seq):
    rows, d = x.shape
    tr = _tile(seq, 256, _SUBLANES)
    row, vec, bvec = _row_specs(rows, d, tr, seq)
    return pl.pallas_call(
        _norm_mod_body,
        grid=(rows // tr,),
        in_specs=[row, vec, bvec, bvec],
        out_specs=row,
        out_shape=jax.ShapeDtypeStruct((rows, d), _BF16),
        compiler_params=_params("parallel"),
        name="norm_mod",
    )(x, g.reshape(1, d), scale, shift)


def _res_norm(x, o, gpost, gate, nxt, seq):
    rows, d = x.shape
    tr = _tile(seq, 256, _SUBLANES)
    row, vec, bvec = _row_specs(rows, d, tr, seq)
    xs = jax.ShapeDtypeStruct((rows, d), _F32)
    if nxt is None:
        return pl.pallas_call(
            functools.partial(_res_norm_body, with_next=False),
            grid=(rows // tr,),
            in_specs=[row, row, vec, bvec],
            out_specs=row,
            out_shape=xs,
            compiler_params=_params("parallel"),
            name="res_norm_last",
        )(x, o, gpost.reshape(1, d), gate), None
    g, scale, shift = nxt
    return pl.pallas_call(
        functools.partial(_res_norm_body, with_next=True),
        grid=(rows // tr,),
        in_specs=[row, row, vec, bvec, vec, bvec, bvec],
        out_specs=(row, row),
        out_shape=(xs, jax.ShapeDtypeStruct((rows, d), _BF16)),
        compiler_params=_params("parallel"),
        name="res_norm",
    )(x, o, gpost.reshape(1, d), gate, g.reshape(1, d), scale, shift)


def _cumsum_body(x_ref, o_ref, *, blk):
    seq = x_ref.shape[0]
    r = lax.broadcasted_iota(jnp.int32, (blk, blk), 0)
    c = lax.broadcasted_iota(jnp.int32, (blk, blk), 1)
    tri = (c <= r).astype(_F32)
    carry = jnp.zeros((1, x_ref.shape[1]), _F32)
    for i in range(seq // blk):
        y = jnp.dot(tri, x_ref[i * blk:(i + 1) * blk, :], precision=_HI,
                    preferred_element_type=_F32) + carry
        o_ref[i * blk:(i + 1) * blk, :] = y
        carry = y[blk - 1:blk, :]


def _cumsum_rows(x, seq):
    rows, n = x.shape
    blk = _tile(seq, 256, _SUBLANES)
    return pl.pallas_call(
        functools.partial(_cumsum_body, blk=blk),
        grid=(rows // seq,),
        in_specs=[pl.BlockSpec((seq, n), lambda b: (b, 0))],
        out_specs=pl.BlockSpec((seq, n), lambda b: (b, 0)),
        out_shape=jax.ShapeDtypeStruct((rows, n), _F32),
        compiler_params=_params("parallel"),
        name="logf_cumsum",
    )(x)


def _fox_prompt_body(q_ref, k_ref, v_ref, cq_ref, ck_ref, o_ref, kb_ref, vb_ref, *, tq, scale):
    qi = pl.program_id(2)

    @pl.when(qi == 0)
    def _():
        kb_ref[...] = k_ref[...].astype(_BF16)
        vb_ref[...] = v_ref[...].astype(_BF16)

    dh = q_ref.shape[1]
    q = (q_ref[...] * scale).astype(_BF16)
    cq = cq_ref[...]

    def step(kj, carry, masked):
        m, l, acc = carry
        k0 = pl.multiple_of(kj * tq, tq)
        k = kb_ref[pl.ds(k0, tq), :]
        v = vb_ref[pl.ds(k0, tq), :]
        s = lax.dot_general(q, k, _NT, preferred_element_type=_F32)
        s = s + (cq - ck_ref[:, pl.ds(k0, tq)])
        if masked:
            row = lax.broadcasted_iota(jnp.int32, (tq, tq), 0)
            col = lax.broadcasted_iota(jnp.int32, (tq, tq), 1)
            s = jnp.where(col <= row, s, -jnp.inf)
        m_new = jnp.maximum(m, jnp.max(s, axis=1, keepdims=True))
        alpha = jnp.exp(m - m_new)
        p = jnp.exp(s - m_new)
        l = alpha * l + jnp.sum(p, axis=1, keepdims=True)
        acc = alpha * acc + jnp.dot(p.astype(_BF16), v, preferred_element_type=_F32)
        return m_new, l, acc

    init = (jnp.full((tq, 1), -jnp.inf, _F32), jnp.zeros((tq, 1), _F32), jnp.zeros((tq, dh), _F32))
    carry = lax.fori_loop(0, qi, lambda kj, c: step(kj, c, False), init)
    _, l, acc = step(qi, carry, True)
    o_ref[...] = (acc / l).astype(o_ref.dtype)


def _fox_prompt(q, k, v, cum_col, cum_row, bsz, seq, heads, dh):
    tq = _tile(seq, 512, _LANES)
    nq = seq // tq
    return pl.pallas_call(
        functools.partial(_fox_prompt_body, tq=tq, scale=dh ** -0.5),
        grid=(bsz, heads, nq),
        in_specs=[pl.BlockSpec((tq, dh), lambda b, h, i: (b * nq + i, h)),
                  pl.BlockSpec((seq, dh), lambda b, h, i: (b, h)),
                  pl.BlockSpec((seq, dh), lambda b, h, i: (b, h)),
                  pl.BlockSpec((None, None, tq, 1), lambda b, h, i: (b, h, i, 0)),
                  pl.BlockSpec((None, None, 1, seq), lambda b, h, i: (b, h, 0, 0))],
        out_specs=pl.BlockSpec((tq, dh), lambda b, h, i: (b * nq + i, h)),
        out_shape=jax.ShapeDtypeStruct((bsz * seq, 2 * heads * dh), _BF16),
        scratch_shapes=[pltpu.VMEM((seq, dh), _BF16), pltpu.VMEM((seq, dh), _BF16)],
        compiler_params=_params("parallel", "parallel", "arbitrary"),
        name="fox_prompt",
    )(q, k, v, cum_col, cum_row)


def _fox_sample_body(pt_ref, qbd_ref, kn_ref, vn_ref, gn_ref, *rest, heads, nq, dh, pps):
    del pt_ref
    kc_refs, vc_refs, gc_refs = rest[:pps], rest[pps:2 * pps], rest[2 * pps:3 * pps]
    o_ref, m_ref, l_ref, acc_ref, carry_ref, lq_ref, kb_ref, vb_ref = rest[3 * pps:]
    step = pl.program_id(1)
    hq = heads * nq
    page = kn_ref.shape[0]
    eye = (lax.broadcasted_iota(jnp.int32, (heads, heads), 0)
           == lax.broadcasted_iota(jnp.int32, (heads, heads), 1)).astype(_F32)
    key = lax.broadcasted_iota(jnp.int32, (heads, page), 1)

    def heads_by_keys(g_ref):
        return lax.dot_general(eye, g_ref[...], _NT, precision=_HI, preferred_element_type=_F32)

    def scan_keys(x, suffix):
        for b in range(page.bit_length() - 1):
            s = 1 << b
            if suffix:
                x = x + jnp.where(key < page - s, pltpu.roll(x, page - s, axis=1), 0.0)
            else:
                x = x + jnp.where(key >= s, pltpu.roll(x, s, axis=1), 0.0)
        return x

    def per_query_rows(x):
        return jnp.broadcast_to(x[:, None, :], (heads, nq, x.shape[1])).reshape(hq, x.shape[1])

    def attend(kb, vb, bias, valid):
        s = lax.dot_general(qbd_ref[...], kb, _NT, preferred_element_type=_F32) + bias
        if valid is not None:
            s = jnp.where(valid, s, -jnp.inf)
        m_prev = m_ref[...]
        m_new = jnp.maximum(m_prev, jnp.max(s, axis=1, keepdims=True))
        alpha = jnp.exp(m_prev - m_new)
        p = jnp.exp(s - m_new)
        l_ref[...] = alpha * l_ref[...] + jnp.sum(p, axis=1, keepdims=True)
        acc_ref[...] = alpha * acc_ref[...] + jnp.dot(p.astype(_BF16), vb, preferred_element_type=_F32)
        m_ref[...] = m_new

    @pl.when(step == 0)
    def _():
        lkeys = per_query_rows(scan_keys(heads_by_keys(gn_ref), suffix=False))
        qrow = lax.broadcasted_iota(jnp.int32, (hq, page), 0) % nq
        lane = lax.broadcasted_iota(jnp.int32, (hq, page), 1)
        lq = jnp.sum(jnp.where(lane == qrow, lkeys, 0.0), axis=1, keepdims=True)
        lq_ref[...] = lq
        carry_ref[...] = jnp.zeros_like(carry_ref)
        m_ref[...] = jnp.full_like(m_ref, -jnp.inf)
        l_ref[...] = jnp.zeros_like(l_ref)
        acc_ref[...] = jnp.zeros_like(acc_ref)
        attend(kn_ref[...].astype(_BF16), vn_ref[...].astype(_BF16), lq - lkeys, lane <= qrow)

    @pl.when(step > 0)
    def _():
        run = carry_ref[...]
        parts = [None] * pps
        for p in reversed(range(pps)):
            gt = heads_by_keys(gc_refs[p])
            incl = scan_keys(gt, suffix=True)
            parts[p] = incl - gt + run
            run = run + incl[:, 0:1]
            kb_ref[p * page:(p + 1) * page, :] = kc_refs[p][...].astype(_BF16)
            vb_ref[p * page:(p + 1) * page, :] = vc_refs[p][...].astype(_BF16)
        carry_ref[...] = run
        later = parts[0] if pps == 1 else jnp.concatenate(parts, axis=1)
        attend(kb_ref[...], vb_ref[...], per_query_rows(later) + lq_ref[...], None)

    @pl.when(step == pl.num_programs(1) - 1)
    def _():
        inv = 1.0 / l_ref[...]
        for h in range(heads):
            rows = slice(h * nq, (h + 1) * nq)
            cols = slice(h * dh, (h + 1) * dh)
            o_ref[:, cols] = (acc_ref[rows, cols] * inv[rows]).astype(o_ref.dtype)


def _fox_sample(qbd, knew, vnew, gnew, cache_k, cache_v, cache_logf, page_table, layer, nq, heads, dh):
    bsz, n_pages = page_table.shape
    ne, n_pool, page = cache_k.shape[:3]
    width = heads * dh
    hq = heads * nq
    ck = cache_k.reshape(ne, n_pool, page, width)
    cv = cache_v.reshape(ne, n_pool, page, width)
    pps = _tile(n_pages, 4, 1)
    nblk = n_pages // pps

    def new_map(b, s, pt):
        return (b, 0, 0)

    def page_map(p):
        def index(b, s, pt):
            return (layer, pt[b * n_pages + (nblk - jnp.maximum(s, 1)) * pps + p], 0, 0)
        return index

    wide = [pl.BlockSpec((None, None, page, width), page_map(p)) for p in range(pps)]
    narrow = [pl.BlockSpec((None, None, page, heads), page_map(p)) for p in range(pps)]
    grid_spec = pltpu.PrefetchScalarGridSpec(
        num_scalar_prefetch=1,
        grid=(bsz, nblk + 1),
        in_specs=[pl.BlockSpec((None, hq, width), new_map),
                  pl.BlockSpec((None, page, width), new_map),
                  pl.BlockSpec((None, page, width), new_map),
                  pl.BlockSpec((None, page, heads), new_map)] + wide + wide + narrow,
        out_specs=pl.BlockSpec((None, nq, width), new_map),
        scratch_shapes=[pltpu.VMEM((hq, 1), _F32), pltpu.VMEM((hq, 1), _F32),
                        pltpu.VMEM((hq, width), _F32), pltpu.VMEM((heads, 1), _F32),
                        pltpu.VMEM((hq, 1), _F32),
                        pltpu.VMEM((pps * page, width), _BF16), pltpu.VMEM((pps * page, width), _BF16)],
    )
    return pl.pallas_call(
        functools.partial(_fox_sample_body, heads=heads, nq=nq, dh=dh, pps=pps),
        grid_spec=grid_spec,
        out_shape=jax.ShapeDtypeStruct((bsz, nq, 2 * width), _BF16),
        compiler_params=_params("parallel", "arbitrary"),
        name="fox_sample",
    )(page_table.reshape(-1), qbd, knew, vnew, gnew, *([ck] * pps), *([cv] * pps), *([cache_logf] * pps))


def _gelu_tanh(y):
    return 0.5 * y * (1.0 + jnp.tanh(math.sqrt(2.0 / math.pi) * (y + 0.044715 * (y * y * y))))


def _s5_body(z_ref, wys_ref, wh_ref, wg_ref, dv_ref, gb_ref, a1_ref, a2_ref, h0_ref, mix_ref,
             o_ref, hf_ref, pin_ref, pout_ref, ybuf_ref, *, gpb, ch, t_len, bsz, nc, ns):
    del mix_ref
    tc = t_len * ch
    rows = bsz * nc
    per = _LANES // ch
    ncol = tc // _LANES
    for t in range(t_len):
        pin_ref[t] = z_ref[pl.ds(t, rows, stride=t_len), :]
    pout_ref[...] = jnp.zeros_like(pout_ref)
    slot = lax.broadcasted_iota(jnp.int32, (rows, _LANES), 1) // ch

    def one(g, _):
        cols = []
        for col in range(ncol):
            acc = jnp.zeros((rows, _LANES), _F32)
            for tt in range(per):
                shift = ((tt - g + per) * ch) % _LANES
                acc = jnp.where(slot == tt, pltpu.roll(pin_ref[col * per + tt], shift, axis=1), acc)
            cols.append(acc)
        u = cols[0] if ncol == 1 else jnp.concatenate(cols, axis=1)
        ub = u.astype(_BF16)
        wys = wys_ref[g]
        yin = jnp.dot(ub, wys[:, :tc].astype(_BF16), preferred_element_type=_F32)
        a1 = a1_ref[g]
        a2 = a2_ref[g]
        if nc == 1:
            st = jnp.dot(u, wys[:, tc:], precision=_HI, preferred_element_type=_F32)
            hstart = h0_ref[g]
            hf_ref[g] = a1[0:1] * hstart + a2[0:1] * pltpu.roll(hstart, ns, axis=1) + st
        else:
            x = jnp.dot(ub, wys[:, tc:].astype(_BF16), preferred_element_type=_F32)
            chunk = lax.broadcasted_iota(jnp.int32, (rows, 2 * ns), 0) % nc
            for k in range(nc.bit_length() - 1):
                sh = 1 << k
                xs = pltpu.roll(x, sh, axis=0)
                upd = a1[k:k + 1] * xs + a2[k:k + 1] * pltpu.roll(xs, ns, axis=1)
                x = x + jnp.where(chunk >= sh, upd, 0.0)
            hstart = jnp.where(chunk >= 1, pltpu.roll(x, 1, axis=0), 0.0)
            for b in range(bsz):
                hf_ref[g, b:b + 1, :] = x[(b + 1) * nc - 1:(b + 1) * nc, :]
        y = yin + jnp.dot(hstart.astype(_BF16), wh_ref[g].astype(_BF16), preferred_element_type=_F32)
        y = _gelu_tanh(y + dv_ref[g] * u)
        gate = jax.nn.sigmoid(jnp.dot(y.astype(_BF16), wg_ref[g].astype(_BF16),
                                      preferred_element_type=_F32) + gb_ref[g])
        out = y * gate
        for col in range(ncol):
            for tt in range(per):
                shift = ((g - tt + per) * ch) % _LANES
                piece = pltpu.roll(out[:, col * _LANES:(col + 1) * _LANES], shift, axis=1)
                t = col * per + tt
                pout_ref[t] = jnp.where(slot == g, piece, pout_ref[t])
        return 0

    lax.fori_loop(0, gpb, one, 0)
    for t in range(t_len):
        ybuf_ref[pl.ds(t, rows, stride=t_len), :] = pout_ref[t]
    o_ref[...] = ybuf_ref[...].astype(o_ref.dtype)


def _s5_operators(lam_re, lam_im, log_step, b_re, b_im, c_re, c_im, d, glu_w, glu_b, t_len, n_scan):
    hp = dict(precision=_HI)
    groups, ns = lam_re.shape
    ch = b_re.shape[-1]
    dt = jnp.exp(log_step)[:, None]

    def power(tau):
        tau = tau.astype(_F32)[None, :, None]
        mag = jnp.exp(lam_re[:, None, :] * dt[:, None, :] * tau)
        ang = lam_im[:, None, :] * dt[:, None, :] * tau
        return mag * jnp.cos(ang), mag * jnp.sin(ang)

    ab_re, ab_im = (p[:, 0] for p in power(jnp.ones((1,))))
    nr, ni = ab_re - 1.0, ab_im
    den = lam_re * lam_re + lam_im * lam_im
    f_re = (nr * lam_re + ni * lam_im) / den
    f_im = (ni * lam_re - nr * lam_im) / den
    bb_re = f_re[..., None] * b_re - f_im[..., None] * b_im
    bb_im = f_re[..., None] * b_im + f_im[..., None] * b_re

    pr, pi = power(jnp.arange(t_len + 1))
    m_re = jnp.einsum('gtp,gop->gtop', pr, c_re, **hp) - jnp.einsum('gtp,gop->gtop', pi, c_im, **hp)
    m_im = jnp.einsum('gtp,gop->gtop', pi, c_re, **hp) + jnp.einsum('gtp,gop->gtop', pr, c_im, **hp)
    lag = (jnp.einsum('gtop,gpi->gtio', m_re[:, :t_len], bb_re, **hp)
           - jnp.einsum('gtop,gpi->gtio', m_im[:, :t_len], bb_im, **hp))
    s_idx = jnp.arange(t_len)[:, None]
    t_idx = jnp.arange(t_len)[None, :]
    toe = lag[:, jnp.maximum(t_idx - s_idx, 0)]
    toe = jnp.where((t_idx >= s_idx)[None, :, :, None, None], toe, 0.0)
    wy = toe.transpose(0, 1, 3, 2, 4).reshape(groups, t_len * ch, t_len * ch)
    er, ei = pr[:, t_len - 1 - jnp.arange(t_len)], pi[:, t_len - 1 - jnp.arange(t_len)]
    ws_re = er[..., None] * bb_re[:, None] - ei[..., None] * bb_im[:, None]
    ws_im = er[..., None] * bb_im[:, None] + ei[..., None] * bb_re[:, None]
    ws = jnp.concatenate([ws_re.transpose(0, 1, 3, 2), ws_im.transpose(0, 1, 3, 2)], axis=-1)
    ws = ws.reshape(groups, t_len * ch, 2 * ns)
    wys = jnp.concatenate([wy, ws], axis=-1)
    wh = jnp.concatenate([m_re[:, 1:].transpose(0, 3, 1, 2), -m_im[:, 1:].transpose(0, 3, 1, 2)], axis=1)
    wh = wh.reshape(groups, 2 * ns, t_len * ch)
    eye = jnp.eye(t_len, dtype=_F32)
    wg = (eye[None, :, None, :, None] * glu_w[:, None, :, None, :]).reshape(groups, t_len * ch, t_len * ch)
    dv = jnp.tile(d, (1, t_len)).reshape(groups, 1, t_len * ch)
    gb = jnp.tile(glu_b, (1, t_len)).reshape(groups, 1, t_len * ch)
    sr, si = power(t_len * (2 ** jnp.arange(_SUBLANES)))
    keep = (jnp.arange(_SUBLANES) < max(n_scan, 1))[None, :, None]
    sr, si = jnp.where(keep, sr, 0.0), jnp.where(keep, si, 0.0)
    a1 = jnp.concatenate([sr, sr], axis=-1)
    a2 = jnp.concatenate([-si, si], axis=-1)
    return wys, wh, wg, dv, gb, a1, a2


def _s5(u, h0, ops, mix, bsz, seq, t_len):
    wys, wh, wg, dv, gb, a1, a2 = ops
    groups, tc, _ = wg.shape
    ch = tc // t_len
    ns2 = wh.shape[1]
    nc = seq // t_len
    rows = bsz * nc
    total = bsz * seq
    assert tc % _LANES == 0 and _LANES % ch == 0
    gpb = _LANES // ch
    if h0 is None:
        h0 = jnp.zeros((groups, bsz, ns2), _F32)
    col0 = mix.shape[1] // 2 // _LANES

    def gmap(i):
        return (i, 0, 0)

    lane_block = pl.BlockSpec((total, _LANES), lambda i: (0, i))
    return pl.pallas_call(
        functools.partial(_s5_body, gpb=gpb, ch=ch, t_len=t_len, bsz=bsz, nc=nc, ns=ns2 // 2),
        grid=(groups // gpb,),
        in_specs=[lane_block,
                  pl.BlockSpec((gpb, tc, tc + ns2), gmap),
                  pl.BlockSpec((gpb, ns2, tc), gmap),
                  pl.BlockSpec((gpb, tc, tc), gmap),
                  pl.BlockSpec((gpb, 1, tc), gmap),
                  pl.BlockSpec((gpb, 1, tc), gmap),
                  pl.BlockSpec((gpb, _SUBLANES, ns2), gmap),
                  pl.BlockSpec((gpb, _SUBLANES, ns2), gmap),
                  pl.BlockSpec((gpb, bsz, ns2), gmap),
                  pl.BlockSpec(memory_space=pl.ANY)],
        out_specs=(pl.BlockSpec((total, _LANES), lambda i: (0, col0 + i)),
                   pl.BlockSpec((gpb, bsz, ns2), gmap)),
        out_shape=(jax.ShapeDtypeStruct(mix.shape, mix.dtype),
                   jax.ShapeDtypeStruct((groups, bsz, ns2), _F32)),
        scratch_shapes=[pltpu.VMEM((t_len, rows, _LANES), _F32),
                        pltpu.VMEM((t_len, rows, _LANES), _F32),
                        pltpu.VMEM((total, _LANES), _F32)],
        input_output_aliases={9: 0},
        compiler_params=_params("arbitrary"),
        name="s5_chunks",
    )(u, wys, wh, wg, dv, gb, a1, a2, h0, mix)


def _conv_body(u_ref, halo_ref, st_ref, w_ref, wb_ref, lg_ref, lb_ref, o_ref, win_ref, sh_ref, y_ref,
               *, nblk, taps, rc, cc):
    i = pl.program_id(0)
    rows, d = u_ref.shape
    hb = halo_ref.shape[0]
    span = hb + rows
    first = (i % nblk) == 0

    @pl.when(first)
    def _():
        win_ref[0:hb, :] = st_ref[...]

    @pl.when(jnp.logical_not(first))
    def _():
        win_ref[0:hb, :] = halo_ref[...]

    win_ref[hb:span, :] = u_ref[...]
    win_ref[span:span + _SUBLANES, :] = jnp.zeros((_SUBLANES, d), _F32)
    off = hb - (taps - 1)

    def col_loop(ci, _):
        c0 = pl.multiple_of(ci * cc, cc)
        for r in range(_SUBLANES):
            sh_ref[r] = win_ref[pl.ds(r, span), pl.ds(c0, cc)]
        bias = wb_ref[:, pl.ds(c0, cc)]
        for r0 in range(0, rows, rc):
            acc = jnp.zeros((rc, cc), _F32) + bias
            for j in range(taps):
                a, r = divmod(off + j, _SUBLANES)
                acc = acc + sh_ref[r, pl.ds(r0 + _SUBLANES * a, rc), :] * w_ref[pl.ds(j, 1), pl.ds(c0, cc)]
            y_ref[r0:r0 + rc, pl.ds(c0, cc)] = acc
        return 0

    lax.fori_loop(0, d // cc, col_loop, 0)
    y = y_ref[...]
    mu = jnp.mean(y, axis=-1, keepdims=True)
    yc = y - mu
    var = jnp.mean(yc * yc, axis=-1, keepdims=True)
    z = yc * lax.rsqrt(var + LN_EPS) * lg_ref[...] + lb_ref[...]
    o_ref[...] = (z * jax.nn.sigmoid(z)).astype(o_ref.dtype)


def _conv_ln_silu(u, state, dw, dw_b, ln_g, ln_b, seq):
    rows_total, d = u.shape
    taps = dw.shape[0]
    hb = -(-(taps - 1) // _SUBLANES) * _SUBLANES
    st = jnp.pad(state.astype(_F32), ((0, 0), (hb - (taps - 1), 0), (0, 0)))
    rows = _tile(seq, 256, hb) if seq % hb == 0 else seq
    nblk = seq // rows
    rc = _tile(rows, 64, _SUBLANES)
    cc = _tile(d, 256, _LANES)
    per = rows // hb if nblk > 1 else 1
    halo_src = u if nblk > 1 else st[0]

    def halo_map(i):
        return (jnp.maximum(i * per - 1, 0) if nblk > 1 else 0, 0)

    vec = pl.BlockSpec((1, d), lambda i: (0, 0))
    return pl.pallas_call(
        functools.partial(_conv_body, nblk=nblk, taps=taps, rc=rc, cc=cc),
        grid=(rows_total // rows,),
        in_specs=[pl.BlockSpec((rows, d), lambda i: (i, 0)),
                  pl.BlockSpec((hb, d), halo_map),
                  pl.BlockSpec((None, hb, d), lambda i: (i // nblk, 0, 0)),
                  pl.BlockSpec((taps, d), lambda i: (0, 0)),
                  vec, vec, vec],
        out_specs=pl.BlockSpec((rows, d), lambda i: (i, 0)),
        out_shape=jax.ShapeDtypeStruct((rows_total, d), _BF16),
        scratch_shapes=[pltpu.VMEM((hb + rows + _SUBLANES, d), _F32),
                        pltpu.VMEM((_SUBLANES, hb + rows, cc), _F32),
                        pltpu.VMEM((rows, d), _F32)],
        compiler_params=_params("parallel"),
        name="conv_ln_silu",
    )(u, halo_src, st, dw, dw_b.reshape(1, d), ln_g.reshape(1, d), ln_b.reshape(1, d))


def kernel(x_prompt, x_sample, cache_k, cache_v, cache_logf, state_s5_re, state_s5_im, state_conv, page_table, c_prompt, c_sample, mod_w, mod_b, norm_mix_pre, norm_mix_post, norm_ffn_pre, norm_ffn_post, ffn_w_gate, ffn_w_up, ffn_w_down, hyb_w_in, hyb_b_f, hyb_w_out, s5_lambda_re, s5_lambda_im, s5_log_step, s5_b_re, s5_b_im, s5_c_re, s5_c_im, s5_d, s5_glu_w, s5_glu_b, conv_w_in, conv_b_in, conv_dw, conv_dw_b, conv_ln_g, conv_ln_b, conv_w_out):
    bp, lp, d = x_prompt.shape
    bs, ls, _ = x_sample.shape
    depth = mod_w.shape[0]
    heads, dh = cache_k.shape[3], cache_k.shape[4]
    fw = heads * dh
    ns = s5_lambda_re.shape[2]
    s5w = s5_b_re.shape[1] * s5_b_re.shape[-1]
    taps = conv_dw.shape[1]
    page = cache_k.shape[2]
    hidden = ffn_w_gate.shape[2]
    seqs = ((bp, lp), (bs, ls))

    n_c = bp + bs
    c_rows = -(-n_c // (2 * _SUBLANES)) * (2 * _SUBLANES)
    c_all = jnp.pad(jnp.concatenate([c_prompt, c_sample], axis=0), ((0, c_rows - n_c), (0, 0)))
    mod = _modulation(c_all, mod_w, mod_b).reshape(depth, c_rows, 6, d)

    def mod_terms(i, grp):
        lo, hi = (0, bp) if grp == 0 else (bp, bp + bs)
        return [mod[i, lo:hi, t][:, None, :] for t in range(6)]

    xs = [x_prompt.reshape(bp * lp, d), x_sample.reshape(bs * ls, d)]
    terms = [mod_terms(0, g) for g in range(2)]
    hs = [_norm_mod(xs[g], norm_mix_pre[0], terms[g][1], terms[g][0], seqs[g][1]) for g in range(2)]

    outs = {k: ([], []) for k in ("k", "v", "f", "s5r", "s5i", "conv")}
    conv_bias = conv_b_in.reshape(conv_b_in.shape[0], 1, conv_b_in.shape[1])

    for i in range(depth):
        j = i // 2
        if i % 2 == 0:
            w_u = hyb_w_in[j][:, 3 * fw + heads:][None]
            w_f = jnp.pad(hyb_w_in[j][:, 3 * fw:3 * fw + heads], ((0, 0), (0, _LANES - heads)))[None]
            b_f = jnp.pad(hyb_b_f[j], (0, _LANES - heads)).reshape(1, 1, _LANES)
            s5_raw = (s5_lambda_re[j], s5_lambda_im[j], s5_log_step[j], s5_b_re[j], s5_b_im[j],
                      s5_c_re[j], s5_c_im[j], s5_d[j], s5_glu_w[j], s5_glu_b[j])
            mixed = []
            for g, (bsz, seq) in enumerate(seqs):
                q, k, v = (_mm(hs[g], [(hyb_w_in, j, part * fw)], [], fw, _ep_id, _F32, 1024, 512, "in_proj_qkv")
                           for part in range(3))
                uz = _mm(hs[g], [(w_u, 0, 0)], [], s5w, _ep_id, _F32, 1024, 512, "in_proj_s5")
                logf = _mm(hs[g], [(w_f, 0, 0)], [(b_f, 0, 0)], _LANES, _ep_logsigmoid, _F32, 1024, _LANES,
                           "forget_gate")
                outs["k"][g].append(k.reshape(bsz, seq, heads, dh))
                outs["v"][g].append(v.reshape(bsz, seq, heads, dh))
                outs["f"][g].append(logf[:, :heads].reshape(bsz, seq, heads))
                if g == 0:
                    cum = _cumsum_rows(logf, seq)[:, :heads].reshape(bsz, seq, heads).transpose(0, 2, 1)
                    mix = _fox_prompt(q, k, v, cum[..., None], cum[:, :, None, :], bsz, seq, heads, dh)
                    t_len = _S5_CHUNK if seq % _S5_CHUNK == 0 else seq
                    h0 = None
                else:
                    qs = q.reshape(bsz, seq, heads, dh) * (dh ** -0.5)
                    eye = jnp.eye(heads, dtype=_F32)
                    qbd = (qs.transpose(0, 2, 1, 3)[:, :, :, None, :] * eye[None, :, None, :, None])
                    qbd = qbd.reshape(bsz, heads * seq, fw).astype(_BF16)
                    pad = ((0, 0), (0, page - seq), (0, 0))
                    knew = jnp.pad(k.reshape(bsz, seq, fw), pad)
                    vnew = jnp.pad(v.reshape(bsz, seq, fw), pad)
                    gnew = jnp.pad(logf[:, :heads].reshape(bsz, seq, heads), pad)
                    mix = _fox_sample(qbd, knew, vnew, gnew, cache_k, cache_v, cache_logf, page_table,
                                      j, seq, heads, dh).reshape(bsz * seq, 2 * fw)
                    t_len = seq
                    h0 = jnp.concatenate([state_s5_re[j], state_s5_im[j]], axis=-1).transpose(1, 0, 2)
                n_scan = (seq // t_len).bit_length() - 1
                ops = _s5_operators(*s5_raw, t_len, n_scan)
                mix, hf = _s5(uz, h0, ops, mix, bsz, seq, t_len)
                hf = hf.transpose(1, 0, 2)
                outs["s5r"][g].append(hf[..., :ns])
                outs["s5i"][g].append(hf[..., ns:])
                mixed.append(_mm(mix, [(hyb_w_out, j, 0)], [], d, _ep_id, _F32, 1024, 512, "out_proj"))
        else:
            mixed = []
            for g, (bsz, seq) in enumerate(seqs):
                u = _mm(hs[g], [(conv_w_in, j, 0), (conv_w_in, j, d)], [(conv_bias, j, 0), (conv_bias, j, d)],
                        d, _ep_glu, _F32, 1024, 256, "conv_in_glu")
                state = jnp.zeros((bsz, taps - 1, d), _F32) if g == 0 else state_conv[j]
                up = jnp.concatenate([state, u.reshape(bsz, seq, d)], axis=1)
                outs["conv"][g].append(up[:, -(taps - 1):])
                y = _conv_ln_silu(u, state, conv_dw[j], conv_dw_b[j], conv_ln_g[j], conv_ln_b[j], seq)
                mixed.append(_mm(y, [(conv_w_out, j, 0)], [], d, _ep_id, _F32, 1024, 512, "conv_out"))

        nxt_terms = [mod_terms(i + 1, g) for g in range(2)] if i + 1 < depth else None
        for g, (bsz, seq) in enumerate(seqs):
            t = terms[g]
            x1, h2 = _res_norm(xs[g], mixed[g], norm_mix_post[i], t[2], (norm_ffn_pre[i], t[4], t[3]), seq)
            a = _mm(h2, [(ffn_w_gate, i, 0), (ffn_w_up, i, 0)], [], hidden, _ep_swiglu, _BF16, 1024, 256,
                    "ffn_gate_up")
            f = _mm(a, [(ffn_w_down, i, 0)], [], d, _ep_id, _F32, 512, 256, "ffn_down", single_buffer_x=True)
            nxt = None if nxt_terms is None else (norm_mix_pre[i + 1], nxt_terms[g][1], nxt_terms[g][0])
            xs[g], hs[g] = _res_norm(x1, f, norm_ffn_post[i], t[5], nxt, seq)
        terms = nxt_terms

    def stack(key, g):
        return jnp.stack(outs[key][g])

    return (xs[0].reshape(bp, lp, d), xs[1].reshape(bs, ls, d),
            stack("k", 0), stack("v", 0), stack("f", 0),
            stack("k", 1), stack("v", 1), stack("f", 1),
            stack("s5r", 0), stack("s5i", 0), stack("s5r", 1), stack("s5i", 1),
            stack("conv", 0), stack("conv", 1))
```

```python
import functools
import math

import jax
import jax.numpy as jnp
from jax import lax
from jax.experimental import pallas as pl
from jax.experimental.pallas import tpu as pltpu

_BF16 = jnp.bfloat16
_F32 = jnp.float32
_HI = lax.Precision.HIGHEST

RMS_EPS = 1e-6
LN_EPS = 1e-5

_V7X_VMEM_BYTES = 64 * 1024 * 1024
_LANES = 128
_SUBLANES = 8
_VMEM_LIMIT = _V7X_VMEM_BYTES - 8 * 1024 * 1024
_S5_CHUNK = 16
_NT = (((1,), (1,)), ((), ()))


def _params(*sem):
    return pltpu.CompilerParams(dimension_semantics=sem, vmem_limit_bytes=_VMEM_LIMIT)


def _tile(dim, pref, align):
    if dim <= pref:
        return dim
    t = (pref // align) * align
    while t >= align:
        if dim % t == 0:
            return t
        t -= align
    return dim


def _mm_body(*refs, n_w, n_b, epilogue):
    x_ref = refs[0]
    w_refs = refs[1:1 + n_w]
    b_refs = refs[1 + n_w:1 + n_w + n_b]
    o_ref = refs[-1]
    x = x_ref[...]
    accs = [jnp.dot(x, w[...].astype(_BF16), preferred_element_type=_F32) for w in w_refs]
    if n_b:
        accs = [a + b[...] for a, b in zip(accs, b_refs)]
    o_ref[...] = epilogue(*accs).astype(o_ref.dtype)


def _mm(x, ws, bs, n, epilogue, out_dtype, tm, tn, name, parts=1):
    m, k = x.shape
    tm = _tile(m, tm, 2 * _SUBLANES)
    tn = _tile(n // parts, tn, _LANES)
    per_part = n // parts // tn

    def spec(rows, layer, col0):
        assert col0 % tn == 0
        return pl.BlockSpec((None, rows, tn), lambda i, j: (layer, 0, col0 // tn + j))

    in_specs = ([pl.BlockSpec((tm, k), lambda i, j: (i, 0))]
                + [spec(k, layer, col0) for _, layer, col0 in ws]
                + [spec(1, layer, col0) for _, layer, col0 in bs])
    out = pl.pallas_call(
        functools.partial(_mm_body, n_w=len(ws), n_b=len(bs), epilogue=epilogue),
        grid=(m // tm, n // tn),
        in_specs=in_specs,
        out_specs=pl.BlockSpec((None, tm, tn), lambda i, j: (j // per_part, i, j % per_part)),
        out_shape=jax.ShapeDtypeStruct((parts, m, n // parts), out_dtype),
        compiler_params=_params("parallel", "arbitrary"),
        name=name,
    )(x, *[w for w, _, _ in ws], *[b for b, _, _ in bs])
    return out if parts > 1 else out[0]


def _cast_body(x_ref, o_ref):
    o_ref[...] = x_ref[...].astype(o_ref.dtype)


def _cast_bf16(w, layer):
    _, r, c = w.shape
    tr = _tile(r, 512, 2 * _SUBLANES)
    return pl.pallas_call(
        _cast_body,
        grid=(r // tr,),
        in_specs=[pl.BlockSpec((None, tr, c), lambda i: (layer, i, 0))],
        out_specs=pl.BlockSpec((None, tr, c), lambda i: (0, i, 0)),
        out_shape=jax.ShapeDtypeStruct((1, r, c), _BF16),
        compiler_params=_params("parallel"),
        name="cast_bf16",
    )(w)


def _ep_id(a):
    return a


def _ep_swiglu(g, u):
    return g * jax.nn.sigmoid(g) * u


def _ep_glu(a, b):
    return a * jax.nn.sigmoid(b)


def _ep_logsigmoid(a):
    return jnp.minimum(a, 0.0) - jnp.log(1.0 + jnp.exp(-jnp.abs(a)))


def _mod_body(c_ref, w_ref, b_ref, o_ref):
    c = c_ref[...]
    a = (c * jax.nn.sigmoid(c)).astype(_BF16)
    o_ref[...] = jnp.dot(a, w_ref[...].astype(_BF16), preferred_element_type=_F32) + b_ref[...]


def _modulation(c, mod_w, mod_b):
    depth, d, n = mod_w.shape
    rows = c.shape[0]
    tn = _tile(n, 512, _LANES)
    return pl.pallas_call(
        _mod_body,
        grid=(depth, n // tn),
        in_specs=[pl.BlockSpec((rows, d), lambda l, j: (0, 0)),
                  pl.BlockSpec((None, d, tn), lambda l, j: (l, 0, j)),
                  pl.BlockSpec((None, 1, tn), lambda l, j: (l, 0, j))],
        out_specs=pl.BlockSpec((None, rows, tn), lambda l, j: (l, 0, j)),
        out_shape=jax.ShapeDtypeStruct((depth, rows, n), _F32),
        compiler_params=_params("parallel", "arbitrary"),
        name="modulation",
    )(c, mod_w, mod_b.reshape(depth, 1, n))


def _rms(x, g):
    return x * lax.rsqrt(jnp.mean(x * x, axis=-1, keepdims=True) + RMS_EPS) * g


def _norm_mod_body(x_ref, g_ref, sc_ref, sh_ref, h_ref):
    y = _rms(x_ref[...], g_ref[...])
    h_ref[...] = (y * (1.0 + sc_ref[...]) + sh_ref[...]).astype(h_ref.dtype)


def _res_norm_body(x_ref, o_ref, gpost_ref, gate_ref, *rest, with_next):
    xn = x_ref[...] + gate_ref[...] * _rms(o_ref[...].astype(_F32), gpost_ref[...])
    if with_next:
        gpre_ref, sc_ref, sh_ref, xn_ref, h_ref = rest
        xn_ref[...] = xn
        h_ref[...] = (_rms(xn, gpre_ref[...]) * (1.0 + sc_ref[...]) + sh_ref[...]).astype(h_ref.dtype)
    else:
        (xn_ref,) = rest
        xn_ref[...] = xn


def _row_specs(rows, d, tr, seq):
    per = seq // tr
    row = pl.BlockSpec((tr, d), lambda i: (i, 0))
    vec = pl.BlockSpec((1, d), lambda i: (0, 0))
    bvec = pl.BlockSpec((None, 1, d), lambda i: (i // per, 0, 0))
    return row, vec, bvec


def _norm_mod(x, g, scale, shift, seq):
    rows, d = x.shape
    tr = _tile(seq, 256, _SUBLANES)
    row, vec, bvec = _row_specs(rows, d, tr, seq)
    return pl.pallas_call(
        _norm_mod_body,
        grid=(rows // tr,),
        in_specs=[row, vec, bvec, bvec],
        out_specs=row,
        out_shape=jax.ShapeDtypeStruct((rows, d), _BF16),
        compiler_params=_params("parallel"),
        name="norm_mod",
    )(x, g.reshape(1, d), scale, shift)


def _res_norm(x, o, gpost, gate, nxt, seq):
    rows, d = x.shape
    tr = _tile(seq, 256, _SUBLANES)
    row, vec, bvec = _row_specs(rows, d, tr, seq)
    xs = jax.ShapeDtypeStruct((rows, d), _F32)
    if nxt is None:
        return pl.pallas_call(
            functools.partial(_res_norm_body, with_next=False),
            grid=(rows // tr,),
            in_specs=[row, row, vec, bvec],
            out_specs=row,
            out_shape=xs,
            compiler_params=_params("parallel"),
            name="res_norm_last",
        )(x, o, gpost.reshape(1, d), gate), None
    g, scale, shift = nxt
    return pl.pallas_call(
        functools.partial(_res_norm_body, with_next=True),
        grid=(rows // tr,),
        in_specs=[row, row, vec, bvec, vec, bvec, bvec],
        out_specs=(row, row),
        out_shape=(xs, jax.ShapeDtypeStruct((rows, d), _BF16)),
        compiler_params=_params("parallel"),
        name="res_norm",
    )(x, o, gpost.reshape(1, d), gate, g.reshape(1, d), scale, shift)


def _cumsum_body(x_ref, o_ref, *, blk):
    seq = x_ref.shape[0]
    r = lax.broadcasted_iota(jnp.int32, (blk, blk), 0)
    c = lax.broadcasted_iota(jnp.int32, (blk, blk), 1)
    tri = (c <= r).astype(_F32)
    carry = jnp.zeros((1, x_ref.shape[1]), _F32)
    for i in range(seq // blk):
        y = jnp.dot(tri, x_ref[i * blk:(i + 1) * blk, :], precision=_HI,
                    preferred_element_type=_F32) + carry
        o_ref[i * blk:(i + 1) * blk, :] = y
        carry = y[blk - 1:blk, :]


def _cumsum_rows(x, seq):
    rows, n = x.shape
    blk = _tile(seq, 256, _SUBLANES)
    return pl.pallas_call(
        functools.partial(_cumsum_body, blk=blk),
        grid=(rows // seq,),
        in_specs=[pl.BlockSpec((seq, n), lambda b: (b, 0))],
        out_specs=pl.BlockSpec((seq, n), lambda b: (b, 0)),
        out_shape=jax.ShapeDtypeStruct((rows, n), _F32),
        compiler_params=_params("parallel"),
        name="logf_cumsum",
    )(x)


def _fox_prompt_body(q_ref, k_ref, v_ref, cq_ref, ck_ref, o_ref, kb_ref, vb_ref, *, tq, scale):
    qi = pl.program_id(2)

    @pl.when(qi == 0)
    def _():
        kb_ref[...] = k_ref[...].astype(_BF16)
        vb_ref[...] = v_ref[...].astype(_BF16)

    dh = q_ref.shape[1]
    q = (q_ref[...] * scale).astype(_BF16)
    cq = cq_ref[...]

    def step(kj, carry, masked):
        m, l, acc = carry
        k0 = pl.multiple_of(kj * tq, tq)
        k = kb_ref[pl.ds(k0, tq), :]
        v = vb_ref[pl.ds(k0, tq), :]
        s = lax.dot_general(q, k, _NT, preferred_element_type=_F32)
        s = s + (cq - ck_ref[:, pl.ds(k0, tq)])
        if masked:
            row = lax.broadcasted_iota(jnp.int32, (tq, tq), 0)
            col = lax.broadcasted_iota(jnp.int32, (tq, tq), 1)
            s = jnp.where(col <= row, s, -jnp.inf)
        m_new = jnp.maximum(m, jnp.max(s, axis=1, keepdims=True))
        alpha = jnp.exp(m - m_new)
        p = jnp.exp(s - m_new)
        l = alpha * l + jnp.sum(p, axis=1, keepdims=True)
        acc = alpha * acc + jnp.dot(p.astype(_BF16), v, preferred_element_type=_F32)
        return m_new, l, acc

    init = (jnp.full((tq, 1), -jnp.inf, _F32), jnp.zeros((tq, 1), _F32), jnp.zeros((tq, dh), _F32))
    carry = lax.fori_loop(0, qi, lambda kj, c: step(kj, c, False), init)
    _, l, acc = step(qi, carry, True)
    o_ref[...] = (acc / l).astype(o_ref.dtype)


def _fox_prompt(q, k, v, cum_col, cum_row, bsz, seq, heads, dh):
    tq = _tile(seq, 512, _LANES)
    nq = seq // tq
    return pl.pallas_call(
        functools.partial(_fox_prompt_body, tq=tq, scale=dh ** -0.5),
        grid=(bsz, heads, nq),
        in_specs=[pl.BlockSpec((tq, dh), lambda b, h, i: (b * nq + i, h)),
                  pl.BlockSpec((seq, dh), lambda b, h, i: (b, h)),
                  pl.BlockSpec((seq, dh), lambda b, h, i: (b, h)),
                  pl.BlockSpec((None, None, tq, 1), lambda b, h, i: (b, h, i, 0)),
                  pl.BlockSpec((None, None, 1, seq), lambda b, h, i: (b, h, 0, 0))],
        out_specs=pl.BlockSpec((tq, dh), lambda b, h, i: (b * nq + i, h)),
        out_shape=jax.ShapeDtypeStruct((bsz * seq, 2 * heads * dh), _BF16),
        scratch_shapes=[pltpu.VMEM((seq, dh), _BF16), pltpu.VMEM((seq, dh), _BF16)],
        compiler_params=_params("parallel", "parallel", "arbitrary"),
        name="fox_prompt",
    )(q, k, v, cum_col, cum_row)


def _fox_sample_body(pt_ref, qbd_ref, kn_ref, vn_ref, gn_ref, *rest, heads, nq, dh, pps):
    del pt_ref
    kc_refs, vc_refs, gc_refs = rest[:pps], rest[pps:2 * pps], rest[2 * pps:3 * pps]
    o_ref, m_ref, l_ref, acc_ref, carry_ref, lq_ref, kb_ref, vb_ref = rest[3 * pps:]
    step = pl.program_id(1)
    hq = heads * nq
    page = kn_ref.shape[0]
    eye = (lax.broadcasted_iota(jnp.int32, (heads, heads), 0)
           == lax.broadcasted_iota(jnp.int32, (heads, heads), 1)).astype(_F32)
    key = lax.broadcasted_iota(jnp.int32, (heads, page), 1)

    def heads_by_keys(g_ref):
        return lax.dot_general(eye, g_ref[...], _NT, precision=_HI, preferred_element_type=_F32)

    def scan_keys(x, suffix):
        for b in range(page.bit_length() - 1):
            s = 1 << b
            if suffix:
                x = x + jnp.where(key < page - s, pltpu.roll(x, page - s, axis=1), 0.0)
            else:
                x = x + jnp.where(key >= s, pltpu.roll(x, s, axis=1), 0.0)
        return x

    def per_query_rows(x):
        return jnp.broadcast_to(x[:, None, :], (heads, nq, x.shape[1])).reshape(hq, x.shape[1])

    def attend(kb, vb, bias, valid):
        s = lax.dot_general(qbd_ref[...], kb, _NT, preferred_element_type=_F32) + bias
        if valid is not None:
            s = jnp.where(valid, s, -jnp.inf)
        m_prev = m_ref[...]
        m_new = jnp.maximum(m_prev, jnp.max(s, axis=1, keepdims=True))
        alpha = jnp.exp(m_prev - m_new)
        p = jnp.exp(s - m_new)
        l_ref[...] = alpha * l_ref[...] + jnp.sum(p, axis=1, keepdims=True)
        acc_ref[...] = alpha * acc_ref[...] + jnp.dot(p.astype(_BF16), vb, preferred_element_type=_F32)
        m_ref[...] = m_new

    @pl.when(step == 0)
    def _():
        lkeys = per_query_rows(scan_keys(heads_by_keys(gn_ref), suffix=False))
        qrow = lax.broadcasted_iota(jnp.int32, (hq, page), 0) % nq
        lane = lax.broadcasted_iota(jnp.int32, (hq, page), 1)
        lq = jnp.sum(jnp.where(lane == qrow, lkeys, 0.0), axis=1, keepdims=True)
        lq_ref[...] = lq
        carry_ref[...] = jnp.zeros_like(carry_ref)
        m_ref[...] = jnp.full_like(m_ref, -jnp.inf)
        l_ref[...] = jnp.zeros_like(l_ref)
        acc_ref[...] = jnp.zeros_like(acc_ref)
        attend(kn_ref[...].astype(_BF16), vn_ref[...].astype(_BF16), lq - lkeys, lane <= qrow)

    @pl.when(step > 0)
    def _():
        run = carry_ref[...]
        parts = [None] * pps
        for p in reversed(range(pps)):
            gt = heads_by_keys(gc_refs[p])
            incl = scan_keys(gt, suffix=True)
            parts[p] = incl - gt + run
            run = run + incl[:, 0:1]
            kb_ref[p * page:(p + 1) * page, :] = kc_refs[p][...].astype(_BF16)
            vb_ref[p * page:(p + 1) * page, :] = vc_refs[p][...].astype(_BF16)
        carry_ref[...] = run
        later = parts[0] if pps == 1 else jnp.concatenate(parts, axis=1)
        attend(kb_ref[...], vb_ref[...], per_query_rows(later) + lq_ref[...], None)

    @pl.when(step == pl.num_programs(1) - 1)
    def _():
        inv = 1.0 / l_ref[...]
        for h in range(heads):
            rows = slice(h * nq, (h + 1) * nq)
            cols = slice(h * dh, (h + 1) * dh)
            o_ref[:, cols] = (acc_ref[rows, cols] * inv[rows]).astype(o_ref.dtype)


def _fox_sample(qbd, knew, vnew, gnew, cache_k, cache_v, cache_logf, page_table, layer, nq, heads, dh):
    bsz, n_pages = page_table.shape
    ne, n_pool, page = cache_k.shape[:3]
    width = heads * dh
    hq = heads * nq
    ck = cache_k.reshape(ne, n_pool, page, width)
    cv = cache_v.reshape(ne, n_pool, page, width)
    pps = _tile(n_pages, 4, 1)
    nblk = n_pages // pps

    def new_map(b, s, pt):
        return (b, 0, 0)

    def page_map(p):
        def index(b, s, pt):
            return (layer, pt[b * n_pages + (nblk - jnp.maximum(s, 1)) * pps + p], 0, 0)
        return index

    wide = [pl.BlockSpec((None, None, page, width), page_map(p)) for p in range(pps)]
    narrow = [pl.BlockSpec((None, None, page, heads), page_map(p)) for p in range(pps)]
    grid_spec = pltpu.PrefetchScalarGridSpec(
        num_scalar_prefetch=1,
        grid=(bsz, nblk + 1),
        in_specs=[pl.BlockSpec((None, hq, width), new_map),
                  pl.BlockSpec((None, page, width), new_map),
                  pl.BlockSpec((None, page, width), new_map),
                  pl.BlockSpec((None, page, heads), new_map)] + wide + wide + narrow,
        out_specs=pl.BlockSpec((None, nq, width), new_map),
        scratch_shapes=[pltpu.VMEM((hq, 1), _F32), pltpu.VMEM((hq, 1), _F32),
                        pltpu.VMEM((hq, width), _F32), pltpu.VMEM((heads, 1), _F32),
                        pltpu.VMEM((hq, 1), _F32),
                        pltpu.VMEM((pps * page, width), _BF16), pltpu.VMEM((pps * page, width), _BF16)],
    )
    return pl.pallas_call(
        functools.partial(_fox_sample_body, heads=heads, nq=nq, dh=dh, pps=pps),
        grid_spec=grid_spec,
        out_shape=jax.ShapeDtypeStruct((bsz, nq, 2 * width), _BF16),
        compiler_params=_params("parallel", "arbitrary"),
        name="fox_sample",
    )(page_table.reshape(-1), qbd, knew, vnew, gnew, *([ck] * pps), *([cv] * pps), *([cache_logf] * pps))


def _gelu_tanh(y):
    return 0.5 * y * (1.0 + jnp.tanh(math.sqrt(2.0 / math.pi) * (y + 0.044715 * (y * y * y))))


def _tile_rows(x, n):
    return jnp.broadcast_to(x[None], (n,) + x.shape).reshape(n * x.shape[0], x.shape[1])


def _repeat_rows(x, n):
    return jnp.broadcast_to(x[:, None, :], (x.shape[0], n, x.shape[1])).reshape(x.shape[0] * n, x.shape[1])


def _s5_chunk_operators(bbr, bbi, cr, ci, crt, cit, prv, piv, prt, pit, glu, *, ch, t_len, ns):
    tc = t_len * ch
    per = _LANES // ch
    bbr_t, bbi_t = _tile_rows(bbr, t_len), _tile_rows(bbi, t_len)
    pr_r, pi_r = _repeat_rows(prv, ch), _repeat_rows(piv, ch)
    x_re = bbr_t * pr_r - bbi_t * pi_r
    x_im = bbr_t * pi_r + bbi_t * pr_r
    ws = jnp.concatenate([x_re, x_im], axis=1)
    k_rep = (lax.dot_general(x_re, _tile_rows(cr, per), _NT, precision=_HI, preferred_element_type=_F32)
             - lax.dot_general(x_im, _tile_rows(ci, per), _NT, precision=_HI, preferred_element_type=_F32))
    k_ext = jnp.concatenate([k_rep, jnp.zeros((tc, _LANES), _F32)], axis=0)
    lane_blk = lax.broadcasted_iota(jnp.int32, (tc, _LANES), 1) // ch
    cols = []
    for col in range(tc // _LANES):
        acc = jnp.zeros((tc, _LANES), _F32)
        for tt in range(per):
            r0 = ch * (t_len - 1 - (col * per + tt))
            acc = jnp.where(lane_blk == tt, k_ext[r0:r0 + tc, :], acc)
        cols.append(acc)
    wy = cols[0] if len(cols) == 1 else jnp.concatenate(cols, axis=1)
    e_mat = (lax.broadcasted_iota(jnp.int32, (t_len, tc), 0)
             == lax.broadcasted_iota(jnp.int32, (t_len, tc), 1) // ch).astype(_F32)
    f_mat = (lax.broadcasted_iota(jnp.int32, (ch, tc), 0)
             == lax.broadcasted_iota(jnp.int32, (ch, tc), 1) % ch).astype(_F32)
    hdot = functools.partial(jnp.dot, precision=_HI, preferred_element_type=_F32)
    pr1, pi1 = hdot(prt, e_mat), hdot(pit, e_mat)
    crx, cix = hdot(crt, f_mat), hdot(cit, f_mat)
    wh = jnp.concatenate([pr1 * crx - pi1 * cix, -(pi1 * crx + pr1 * cix)], axis=0)
    same_t = (lax.broadcasted_iota(jnp.int32, (tc, tc), 0) // ch
              == lax.broadcasted_iota(jnp.int32, (tc, tc), 1) // ch)
    wg = jnp.where(same_t, _tile_rows(hdot(glu, f_mat), t_len), 0.0)
    return wy, ws, wh, wg


def _s5_body(z_ref, bbr_ref, bbi_ref, cr_ref, ci_ref, crt_ref, cit_ref, prv_ref, piv_ref, prt_ref, pit_ref,
             glu_ref, dv_ref, gb_ref, a1_ref, a2_ref, h0_ref, mix_ref,
             o_ref, hf_ref, pin_ref, pout_ref, ybuf_ref, *, gpb, ch, t_len, bsz, nc, ns):
    del mix_ref
    tc = t_len * ch
    rows = bsz * nc
    per = _LANES // ch
    ncol = tc // _LANES
    for t in range(t_len):
        pin_ref[t] = z_ref[pl.ds(t, rows, stride=t_len), :]
    pout_ref[...] = jnp.zeros_like(pout_ref)
    slot = lax.broadcasted_iota(jnp.int32, (rows, _LANES), 1) // ch

    def one(g, _):
        wy, ws, wh, wg = _s5_chunk_operators(
            bbr_ref[g], bbi_ref[g], cr_ref[g], ci_ref[g], crt_ref[g], cit_ref[g], prv_ref[g], piv_ref[g],
            prt_ref[g], pit_ref[g], glu_ref[g], ch=ch, t_len=t_len, ns=ns)
        cols = []
        for col in range(ncol):
            acc = jnp.zeros((rows, _LANES), _F32)
            for tt in range(per):
                shift = ((tt - g + per) * ch) % _LANES
                acc = jnp.where(slot == tt, pltpu.roll(pin_ref[col * per + tt], shift, axis=1), acc)
            cols.append(acc)
        u = cols[0] if ncol == 1 else jnp.concatenate(cols, axis=1)
        ub = u.astype(_BF16)
        yin = jnp.dot(ub, wy.astype(_BF16), preferred_element_type=_F32)
        a1 = a1_ref[g]
        a2 = a2_ref[g]
        if nc == 1:
            st = jnp.dot(u, ws, precision=_HI, preferred_element_type=_F32)
            hstart = h0_ref[g]
            hf_ref[g] = a1[0:1] * hstart + a2[0:1] * pltpu.roll(hstart, ns, axis=1) + st
        else:
            x = jnp.dot(ub, ws.astype(_BF16), preferred_element_type=_F32)
            chunk = lax.broadcasted_iota(jnp.int32, (rows, 2 * ns), 0) % nc
            for k in range(nc.bit_length() - 1):
                sh = 1 << k
                xs = pltpu.roll(x, sh, axis=0)
                upd = a1[k:k + 1] * xs + a2[k:k + 1] * pltpu.roll(xs, ns, axis=1)
                x = x + jnp.where(chunk >= sh, upd, 0.0)
            hstart = jnp.where(chunk >= 1, pltpu.roll(x, 1, axis=0), 0.0)
            for b in range(bsz):
                hf_ref[g, b:b + 1, :] = x[(b + 1) * nc - 1:(b + 1) * nc, :]
        y = yin + jnp.dot(hstart.astype(_BF16), wh.astype(_BF16), preferred_element_type=_F32)
        y = _gelu_tanh(y + dv_ref[g] * u)
        gate = jax.nn.sigmoid(jnp.dot(y.astype(_BF16), wg.astype(_BF16),
                                      preferred_element_type=_F32) + gb_ref[g])
        out = y * gate
        for col in range(ncol):
            for tt in range(per):
                shift = ((g - tt + per) * ch) % _LANES
                piece = pltpu.roll(out[:, col * _LANES:(col + 1) * _LANES], shift, axis=1)
                t = col * per + tt
                pout_ref[t] = jnp.where(slot == g, piece, pout_ref[t])
        return 0

    lax.fori_loop(0, gpb, one, 0)
    for t in range(t_len):
        ybuf_ref[pl.ds(t, rows, stride=t_len), :] = pout_ref[t]
    o_ref[...] = ybuf_ref[...].astype(o_ref.dtype)


def _s5_operators(lam_re, lam_im, log_step, b_re, b_im, c_re, c_im, d, glu_w, glu_b, t_len, n_scan):
    groups, ns = lam_re.shape
    dt = jnp.exp(log_step)[:, None]

    def power(tau):
        tau = tau.astype(_F32)[None, :, None]
        mag = jnp.exp(lam_re[:, None, :] * dt[:, None, :] * tau)
        ang = lam_im[:, None, :] * dt[:, None, :] * tau
        return mag * jnp.cos(ang), mag * jnp.sin(ang)

    ab_re, ab_im = (p[:, 0] for p in power(jnp.ones((1,))))
    nr, ni = ab_re - 1.0, ab_im
    den = lam_re * lam_re + lam_im * lam_im
    f_re = (nr * lam_re + ni * lam_im) / den
    f_im = (ni * lam_re - nr * lam_im) / den
    bb_re = f_re[..., None] * b_re - f_im[..., None] * b_im
    bb_im = f_re[..., None] * b_im + f_im[..., None] * b_re
    pr, pi = power(jnp.arange(t_len + 1))
    rev = t_len - 1 - jnp.arange(t_len)
    dv = jnp.tile(d, (1, t_len))[:, None, :]
    gb = jnp.tile(glu_b, (1, t_len))[:, None, :]
    sr, si = power(t_len * (2 ** jnp.arange(_SUBLANES)))
    keep = (jnp.arange(_SUBLANES) < max(n_scan, 1))[None, :, None]
    sr, si = jnp.where(keep, sr, 0.0), jnp.where(keep, si, 0.0)
    a1 = jnp.concatenate([sr, sr], axis=-1)
    a2 = jnp.concatenate([-si, si], axis=-1)
    return (bb_re.transpose(0, 2, 1), bb_im.transpose(0, 2, 1), c_re, c_im,
            c_re.transpose(0, 2, 1), c_im.transpose(0, 2, 1), pr[:, rev], pi[:, rev],
            pr[:, 1:].transpose(0, 2, 1), pi[:, 1:].transpose(0, 2, 1), glu_w, dv, gb, a1, a2)


def _s5(u, h0, ops, mix, bsz, seq, t_len):
    groups, ch, ns = ops[0].shape
    ns2 = 2 * ns
    tc = t_len * ch
    nc = seq // t_len
    rows = bsz * nc
    total = bsz * seq
    assert tc % _LANES == 0 and _LANES % ch == 0
    gpb = _LANES // ch
    if h0 is None:
        h0 = jnp.zeros((groups, bsz, ns2), _F32)
    col0 = mix.shape[1] // 2 // _LANES

    def gmap(i):
        return (i, 0, 0)

    def per_group(a):
        return pl.BlockSpec((gpb,) + a.shape[1:], gmap)

    return pl.pallas_call(
        functools.partial(_s5_body, gpb=gpb, ch=ch, t_len=t_len, bsz=bsz, nc=nc, ns=ns),
        grid=(groups // gpb,),
        in_specs=([pl.BlockSpec((total, _LANES), lambda i: (0, i))]
                  + [per_group(a) for a in ops] + [per_group(h0), pl.BlockSpec(memory_space=pl.ANY)]),
        out_specs=(pl.BlockSpec((total, _LANES), lambda i: (0, col0 + i)),
                   pl.BlockSpec((gpb, bsz, ns2), gmap)),
        out_shape=(jax.ShapeDtypeStruct(mix.shape, mix.dtype),
                   jax.ShapeDtypeStruct((groups, bsz, ns2), _F32)),
        scratch_shapes=[pltpu.VMEM((t_len, rows, _LANES), _F32),
                        pltpu.VMEM((t_len, rows, _LANES), _F32),
                        pltpu.VMEM((total, _LANES), _F32)],
        input_output_aliases={len(ops) + 2: 0},
        compiler_params=_params("arbitrary"),
        name="s5_chunks",
    )(u, *ops, h0, mix)


def _conv_body(u_ref, halo_ref, st_ref, w_ref, wb_ref, lg_ref, lb_ref, o_ref, win_ref, sh_ref, y_ref,
               *, nblk, taps, rc, cc):
    i = pl.program_id(0)
    rows, d = u_ref.shape
    hb = halo_ref.shape[0]
    span = hb + rows
    first = (i % nblk) == 0

    @pl.when(first)
    def _():
        win_ref[0:hb, :] = st_ref[...]

    @pl.when(jnp.logical_not(first))
    def _():
        win_ref[0:hb, :] = halo_ref[...]

    win_ref[hb:span, :] = u_ref[...]
    win_ref[span:span + _SUBLANES, :] = jnp.zeros((_SUBLANES, d), _F32)
    off = hb - (taps - 1)

    def col_loop(ci, _):
        c0 = pl.multiple_of(ci * cc, cc)
        for r in range(_SUBLANES):
            sh_ref[r] = win_ref[pl.ds(r, span), pl.ds(c0, cc)]
        bias = wb_ref[:, pl.ds(c0, cc)]
        for r0 in range(0, rows, rc):
            acc = jnp.zeros((rc, cc), _F32) + bias
            for j in range(taps):
                a, r = divmod(off + j, _SUBLANES)
                acc = acc + sh_ref[r, pl.ds(r0 + _SUBLANES * a, rc), :] * w_ref[pl.ds(j, 1), pl.ds(c0, cc)]
            y_ref[r0:r0 + rc, pl.ds(c0, cc)] = acc
        return 0

    lax.fori_loop(0, d // cc, col_loop, 0)
    y = y_ref[...]
    mu = jnp.mean(y, axis=-1, keepdims=True)
    yc = y - mu
    var = jnp.mean(yc * yc, axis=-1, keepdims=True)
    z = yc * lax.rsqrt(var + LN_EPS) * lg_ref[...] + lb_ref[...]
    o_ref[...] = (z * jax.nn.sigmoid(z)).astype(o_ref.dtype)


def _conv_ln_silu(u, state, dw, dw_b, ln_g, ln_b, seq):
    rows_total, d = u.shape
    taps = dw.shape[0]
    hb = -(-(taps - 1) // _SUBLANES) * _SUBLANES
    st = jnp.pad(state.astype(_F32), ((0, 0), (hb - (taps - 1), 0), (0, 0)))
    rows = _tile(seq, 256, hb) if seq % hb == 0 else seq
    nblk = seq // rows
    rc = _tile(rows, 64, _SUBLANES)
    cc = _tile(d, 256, _LANES)
    per = rows // hb if nblk > 1 else 1
    halo_src = u if nblk > 1 else st[0]

    def halo_map(i):
        return (jnp.maximum(i * per - 1, 0) if nblk > 1 else 0, 0)

    vec = pl.BlockSpec((1, d), lambda i: (0, 0))
    return pl.pallas_call(
        functools.partial(_conv_body, nblk=nblk, taps=taps, rc=rc, cc=cc),
        grid=(rows_total // rows,),
        in_specs=[pl.BlockSpec((rows, d), lambda i: (i, 0)),
                  pl.BlockSpec((hb, d), halo_map),
                  pl.BlockSpec((None, hb, d), lambda i: (i // nblk, 0, 0)),
                  pl.BlockSpec((taps, d), lambda i: (0, 0)),
                  vec, vec, vec],
        out_specs=pl.BlockSpec((rows, d), lambda i: (i, 0)),
        out_shape=jax.ShapeDtypeStruct((rows_total, d), _BF16),
        scratch_shapes=[pltpu.VMEM((hb + rows + _SUBLANES, d), _F32),
                        pltpu.VMEM((_SUBLANES, hb + rows, cc), _F32),
                        pltpu.VMEM((rows, d), _F32)],
        compiler_params=_params("parallel"),
        name="conv_ln_silu",
    )(u, halo_src, st, dw, dw_b.reshape(1, d), ln_g.reshape(1, d), ln_b.reshape(1, d))


def kernel(x_prompt, x_sample, cache_k, cache_v, cache_logf, state_s5_re, state_s5_im, state_conv, page_table, c_prompt, c_sample, mod_w, mod_b, norm_mix_pre, norm_mix_post, norm_ffn_pre, norm_ffn_post, ffn_w_gate, ffn_w_up, ffn_w_down, hyb_w_in, hyb_b_f, hyb_w_out, s5_lambda_re, s5_lambda_im, s5_log_step, s5_b_re, s5_b_im, s5_c_re, s5_c_im, s5_d, s5_glu_w, s5_glu_b, conv_w_in, conv_b_in, conv_dw, conv_dw_b, conv_ln_g, conv_ln_b, conv_w_out):
    bp, lp, d = x_prompt.shape
    bs, ls, _ = x_sample.shape
    depth = mod_w.shape[0]
    heads, dh = cache_k.shape[3], cache_k.shape[4]
    fw = heads * dh
    ns = s5_lambda_re.shape[2]
    s5w = s5_b_re.shape[1] * s5_b_re.shape[-1]
    taps = conv_dw.shape[1]
    page = cache_k.shape[2]
    hidden = ffn_w_gate.shape[2]
    seqs = ((bp, lp), (bs, ls))

    n_c = bp + bs
    c_rows = -(-n_c // (2 * _SUBLANES)) * (2 * _SUBLANES)
    c_all = jnp.pad(jnp.concatenate([c_prompt, c_sample], axis=0), ((0, c_rows - n_c), (0, 0)))
    mod = _modulation(c_all, mod_w, mod_b).reshape(depth, c_rows, 6, d)

    def mod_terms(i, grp):
        lo, hi = (0, bp) if grp == 0 else (bp, bp + bs)
        return [mod[i, lo:hi, t][:, None, :] for t in range(6)]

    xs = [x_prompt.reshape(bp * lp, d), x_sample.reshape(bs * ls, d)]
    terms = [mod_terms(0, g) for g in range(2)]
    hs = [_norm_mod(xs[g], norm_mix_pre[0], terms[g][1], terms[g][0], seqs[g][1]) for g in range(2)]

    outs = {k: ([], []) for k in ("k", "v", "f", "s5r", "s5i", "conv")}
    conv_bias = conv_b_in.reshape(conv_b_in.shape[0], 1, conv_b_in.shape[1])

    for i in range(depth):
        j = i // 2
        if i % 2 == 0:
            w_u = hyb_w_in[j][:, 3 * fw + heads:][None]
            w_f = jnp.pad(hyb_w_in[j][:, 3 * fw:3 * fw + heads], ((0, 0), (0, _LANES - heads)))[None]
            b_f = jnp.pad(hyb_b_f[j], (0, _LANES - heads)).reshape(1, 1, _LANES)
            s5_raw = (s5_lambda_re[j], s5_lambda_im[j], s5_log_step[j], s5_b_re[j], s5_b_im[j],
                      s5_c_re[j], s5_c_im[j], s5_d[j], s5_glu_w[j], s5_glu_b[j])
            mixed = []
            for g, (bsz, seq) in enumerate(seqs):
                q, k, v = _mm(hs[g], [(hyb_w_in, j, 0)], [], 3 * fw, _ep_id, _F32, 1024, 512, "in_proj_qkv",
                              parts=3)
                uz = _mm(hs[g], [(w_u, 0, 0)], [], s5w, _ep_id, _F32, 1024, 512, "in_proj_s5")
                logf = _mm(hs[g], [(w_f, 0, 0)], [(b_f, 0, 0)], _LANES, _ep_logsigmoid, _F32, 1024, _LANES,
                           "forget_gate")
                outs["k"][g].append(k.reshape(bsz, seq, heads, dh))
                outs["v"][g].append(v.reshape(bsz, seq, heads, dh))
                outs["f"][g].append(logf[:, :heads].reshape(bsz, seq, heads))
                if g == 0:
                    cum = _cumsum_rows(logf, seq)[:, :heads].reshape(bsz, seq, heads).transpose(0, 2, 1)
                    mix = _fox_prompt(q, k, v, cum[..., None], cum[:, :, None, :], bsz, seq, heads, dh)
                    t_len = _S5_CHUNK if seq % _S5_CHUNK == 0 else seq
                    h0 = None
                else:
                    qs = q.reshape(bsz, seq, heads, dh) * (dh ** -0.5)
                    eye = jnp.eye(heads, dtype=_F32)
                    qbd = (qs.transpose(0, 2, 1, 3)[:, :, :, None, :] * eye[None, :, None, :, None])
                    qbd = qbd.reshape(bsz, heads * seq, fw).astype(_BF16)
                    pad = ((0, 0), (0, page - seq), (0, 0))
                    knew = jnp.pad(k.reshape(bsz, seq, fw), pad)
                    vnew = jnp.pad(v.reshape(bsz, seq, fw), pad)
                    gnew = jnp.pad(logf[:, :heads].reshape(bsz, seq, heads), pad)
                    mix = _fox_sample(qbd, knew, vnew, gnew, cache_k, cache_v, cache_logf, page_table,
                                      j, seq, heads, dh).reshape(bsz * seq, 2 * fw)
                    t_len = seq
                    h0 = jnp.concatenate([state_s5_re[j], state_s5_im[j]], axis=-1).transpose(1, 0, 2)
                n_scan = (seq // t_len).bit_length() - 1
                ops = _s5_operators(*s5_raw, t_len, n_scan)
                mix, hf = _s5(uz, h0, ops, mix, bsz, seq, t_len)
                hf = hf.transpose(1, 0, 2)
                outs["s5r"][g].append(hf[..., :ns])
                outs["s5i"][g].append(hf[..., ns:])
                mixed.append(_mm(mix, [(hyb_w_out, j, 0)], [], d, _ep_id, _F32, 1024, 512, "out_proj"))
        else:
            mixed = []
            for g, (bsz, seq) in enumerate(seqs):
                u = _mm(hs[g], [(conv_w_in, j, 0), (conv_w_in, j, d)], [(conv_bias, j, 0), (conv_bias, j, d)],
                        d, _ep_glu, _F32, 1024, 256, "conv_in_glu")
                state = jnp.zeros((bsz, taps - 1, d), _F32) if g == 0 else state_conv[j]
                u3 = u.reshape(bsz, seq, d)
                if seq >= taps - 1:
                    outs["conv"][g].append(u3[:, seq - (taps - 1):])
                else:
                    outs["conv"][g].append(jnp.concatenate([state[:, seq:], u3], axis=1))
                y = _conv_ln_silu(u, state, conv_dw[j], conv_dw_b[j], conv_ln_g[j], conv_ln_b[j], seq)
                mixed.append(_mm(y, [(conv_w_out, j, 0)], [], d, _ep_id, _F32, 1024, 512, "conv_out"))

        nxt_terms = [mod_terms(i + 1, g) for g in range(2)] if i + 1 < depth else None
        w_down = _cast_bf16(ffn_w_down, i)
        for g, (bsz, seq) in enumerate(seqs):
            t = terms[g]
            x1, h2 = _res_norm(xs[g], mixed[g], norm_mix_post[i], t[2], (norm_ffn_pre[i], t[4], t[3]), seq)
            a = _mm(h2, [(ffn_w_gate, i, 0), (ffn_w_up, i, 0)], [], hidden, _ep_swiglu, _BF16, 1024, 256,
                    "ffn_gate_up")
            f = _mm(a, [(w_down, 0, 0)], [], d, _ep_id, _F32, 512, 256, "ffn_down")
            nxt = None if nxt_terms is None else (norm_mix_pre[i + 1], nxt_terms[g][1], nxt_terms[g][0])
            xs[g], hs[g] = _res_norm(x1, f, norm_ffn_post[i], t[5], nxt, seq)
        terms = nxt_terms

    def stack(key, g):
        return jnp.stack(outs[key][g])

    return (xs[0].reshape(bp, lp, d), xs[1].reshape(bs, ls, d),
            stack("k", 0), stack("v", 0), stack("f", 0),
            stack("k", 1), stack("v", 1), stack("f", 1),
            stack("s5r", 0), stack("s5i", 0), stack("s5r", 1), stack("s5i", 1),
            stack("conv", 0), stack("conv", 1))
```

```python
import functools
import math

import jax
import jax.numpy as jnp
from jax import lax
from jax.experimental import pallas as pl
from jax.experimental.pallas import tpu as pltpu

_BF16 = jnp.bfloat16
_F32 = jnp.float32
_HI = lax.Precision.HIGHEST

RMS_EPS = 1e-6
LN_EPS = 1e-5

_V7X_VMEM_BYTES = 64 * 1024 * 1024
_LANES = 128
_SUBLANES = 8
_VMEM_LIMIT = _V7X_VMEM_BYTES - 8 * 1024 * 1024
_S5_CHUNK = 16
_NT = (((1,), (1,)), ((), ()))


def _params(*sem):
    return pltpu.CompilerParams(dimension_semantics=sem, vmem_limit_bytes=_VMEM_LIMIT)


def _tile(dim, pref, align):
    if dim <= pref:
        return dim
    t = (pref // align) * align
    while t >= align:
        if dim % t == 0:
            return t
        t -= align
    return dim


def _mm_body(*refs, n_w, n_b, epilogue):
    x_ref = refs[0]
    w_refs = refs[1:1 + n_w]
    b_refs = refs[1 + n_w:1 + n_w + n_b]
    o_ref = refs[-1]
    x = x_ref[...]
    accs = [jnp.dot(x, w[...].astype(_BF16), preferred_element_type=_F32) for w in w_refs]
    if n_b:
        accs = [a + b[...] for a, b in zip(accs, b_refs)]
    o_ref[...] = epilogue(*accs).astype(o_ref.dtype)


def _mm(x, ws, bs, n, epilogue, out_dtype, tm, tn, name, parts=1):
    m, k = x.shape
    tm = _tile(m, tm, 2 * _SUBLANES)
    tn = _tile(n // parts, tn, _LANES)
    per_part = n // parts // tn

    def spec(rows, layer, col0):
        assert col0 % tn == 0
        return pl.BlockSpec((None, rows, tn), lambda i, j: (layer, 0, col0 // tn + j))

    in_specs = ([pl.BlockSpec((tm, k), lambda i, j: (i, 0))]
                + [spec(k, layer, col0) for _, layer, col0 in ws]
                + [spec(1, layer, col0) for _, layer, col0 in bs])
    out = pl.pallas_call(
        functools.partial(_mm_body, n_w=len(ws), n_b=len(bs), epilogue=epilogue),
        grid=(m // tm, n // tn),
        in_specs=in_specs,
        out_specs=pl.BlockSpec((None, tm, tn), lambda i, j: (j // per_part, i, j % per_part)),
        out_shape=jax.ShapeDtypeStruct((parts, m, n // parts), out_dtype),
        compiler_params=_params("parallel", "arbitrary"),
        name=name,
    )(x, *[w for w, _, _ in ws], *[b for b, _, _ in bs])
    return out if parts > 1 else out[0]


def _cast_body(x_ref, o_ref):
    o_ref[...] = x_ref[...].astype(o_ref.dtype)


def _cast_bf16(w, layer):
    _, r, c = w.shape
    tr = _tile(r, 512, 2 * _SUBLANES)
    return pl.pallas_call(
        _cast_body,
        grid=(r // tr,),
        in_specs=[pl.BlockSpec((None, tr, c), lambda i: (layer, i, 0))],
        out_specs=pl.BlockSpec((None, tr, c), lambda i: (0, i, 0)),
        out_shape=jax.ShapeDtypeStruct((1, r, c), _BF16),
        compiler_params=_params("parallel"),
        name="cast_bf16",
    )(w)


def _ep_id(a):
    return a


def _ep_swiglu(g, u):
    return g * jax.nn.sigmoid(g) * u


def _ep_glu(a, b):
    return a * jax.nn.sigmoid(b)


def _ep_logsigmoid(a):
    return jnp.minimum(a, 0.0) - jnp.log(1.0 + jnp.exp(-jnp.abs(a)))


def _mod_body(c_ref, w_ref, b_ref, o_ref):
    c = c_ref[...]
    a = (c * jax.nn.sigmoid(c)).astype(_BF16)
    o_ref[...] = jnp.dot(a, w_ref[...].astype(_BF16), preferred_element_type=_F32) + b_ref[...]


def _modulation(c, mod_w, mod_b):
    depth, d, n = mod_w.shape
    rows = c.shape[0]
    tn = _tile(n, 512, _LANES)
    return pl.pallas_call(
        _mod_body,
        grid=(depth, n // tn),
        in_specs=[pl.BlockSpec((rows, d), lambda l, j: (0, 0)),
                  pl.BlockSpec((None, d, tn), lambda l, j: (l, 0, j)),
                  pl.BlockSpec((None, 1, tn), lambda l, j: (l, 0, j))],
        out_specs=pl.BlockSpec((None, rows, tn), lambda l, j: (l, 0, j)),
        out_shape=jax.ShapeDtypeStruct((depth, rows, n), _F32),
        compiler_params=_params("parallel", "arbitrary"),
        name="modulation",
    )(c, mod_w, mod_b.reshape(depth, 1, n))


def _rms(x, g):
    return x * lax.rsqrt(jnp.mean(x * x, axis=-1, keepdims=True) + RMS_EPS) * g


def _norm_mod_body(x_ref, g_ref, sc_ref, sh_ref, h_ref):
    y = _rms(x_ref[...], g_ref[...])
    h_ref[...] = (y * (1.0 + sc_ref[...]) + sh_ref[...]).astype(h_ref.dtype)


def _res_norm_body(x_ref, o_ref, gpost_ref, gate_ref, *rest, with_next):
    xn = x_ref[...] + gate_ref[...] * _rms(o_ref[...].astype(_F32), gpost_ref[...])
    if with_next:
        gpre_ref, sc_ref, sh_ref, xn_ref, h_ref = rest
        xn_ref[...] = xn
        h_ref[...] = (_rms(xn, gpre_ref[...]) * (1.0 + sc_ref[...]) + sh_ref[...]).astype(h_ref.dtype)
    else:
        (xn_ref,) = rest
        xn_ref[...] = xn


def _row_specs(rows, d, tr, seq):
    per = seq // tr
    row = pl.BlockSpec((tr, d), lambda i: (i, 0))
    vec = pl.BlockSpec((1, d), lambda i: (0, 0))
    bvec = pl.BlockSpec((None, 1, d), lambda i: (i // per, 0, 0))
    return row, vec, bvec


def _norm_mod(x, g, scale, shift, seq):
    rows, d = x.shape
    tr = _tile(seq, 256, _SUBLANES)
    row, vec, bvec = _row_specs(rows, d, tr, seq)
    return pl.pallas_call(
        _norm_mod_body,
        grid=(rows // tr,),
        in_specs=[row, vec, bvec, bvec],
        out_specs=row,
        out_shape=jax.ShapeDtypeStruct((rows, d), _BF16),
        compiler_params=_params("parallel"),
        name="norm_mod",
    )(x, g.reshape(1, d), scale, shift)


def _res_norm(x, o, gpost, gate, nxt, seq):
    rows, d = x.shape
    tr = _tile(seq, 256, _SUBLANES)
    row, vec, bvec = _row_specs(rows, d, tr, seq)
    xs = jax.ShapeDtypeStruct((rows, d), _F32)
    if nxt is None:
        return pl.pallas_call(
            functools.partial(_res_norm_body, with_next=False),
            grid=(rows // tr,),
            in_specs=[row, row, vec, bvec],
            out_specs=row,
            out_shape=xs,
            compiler_params=_params("parallel"),
            name="res_norm_last",
        )(x, o, gpost.reshape(1, d), gate), None
    g, scale, shift = nxt
    return pl.pallas_call(
        functools.partial(_res_norm_body, with_next=True),
        grid=(rows // tr,),
        in_specs=[row, row, vec, bvec, vec, bvec, bvec],
        out_specs=(row, row),
        out_shape=(xs, jax.ShapeDtypeStruct((rows, d), _BF16)),
        compiler_params=_params("parallel"),
        name="res_norm",
    )(x, o, gpost.reshape(1, d), gate, g.reshape(1, d), scale, shift)


def _cumsum_body(x_ref, o_ref, *, blk):
    seq = x_ref.shape[0]
    r = lax.broadcasted_iota(jnp.int32, (blk, blk), 0)
    c = lax.broadcasted_iota(jnp.int32, (blk, blk), 1)
    tri = (c <= r).astype(_F32)
    carry = jnp.zeros((1, x_ref.shape[1]), _F32)
    for i in range(seq // blk):
        y = jnp.dot(tri, x_ref[i * blk:(i + 1) * blk, :], precision=_HI,
                    preferred_element_type=_F32) + carry
        o_ref[i * blk:(i + 1) * blk, :] = y
        carry = y[blk - 1:blk, :]


def _cumsum_rows(x, seq):
    rows, n = x.shape
    blk = _tile(seq, 256, _SUBLANES)
    return pl.pallas_call(
        functools.partial(_cumsum_body, blk=blk),
        grid=(rows // seq,),
        in_specs=[pl.BlockSpec((seq, n), lambda b: (b, 0))],
        out_specs=pl.BlockSpec((seq, n), lambda b: (b, 0)),
        out_shape=jax.ShapeDtypeStruct((rows, n), _F32),
        compiler_params=_params("parallel"),
        name="logf_cumsum",
    )(x)


def _fox_prompt_body(q_ref, k_ref, v_ref, cq_ref, ck_ref, o_ref, kb_ref, vb_ref, *, tq, scale):
    h = pl.program_id(1)
    qi = pl.program_id(2)

    @pl.when(qi == 0)
    def _():
        kb_ref[...] = k_ref[...].astype(_BF16)
        vb_ref[...] = v_ref[...].astype(_BF16)

    dh = q_ref.shape[1]
    q = (q_ref[...] * scale).astype(_BF16)
    head_lane = lax.broadcasted_iota(jnp.int32, cq_ref.shape, 1) == h
    cq = jnp.sum(jnp.where(head_lane, cq_ref[...], 0.0), axis=1, keepdims=True)

    def step(kj, carry, masked):
        m, l, acc = carry
        k0 = pl.multiple_of(kj * tq, tq)
        k = kb_ref[pl.ds(k0, tq), :]
        v = vb_ref[pl.ds(k0, tq), :]
        s = lax.dot_general(q, k, _NT, preferred_element_type=_F32)
        s = s + (cq - ck_ref[pl.ds(h, 1), pl.ds(k0, tq)])
        if masked:
            row = lax.broadcasted_iota(jnp.int32, (tq, tq), 0)
            col = lax.broadcasted_iota(jnp.int32, (tq, tq), 1)
            s = jnp.where(col <= row, s, -jnp.inf)
        m_new = jnp.maximum(m, jnp.max(s, axis=1, keepdims=True))
        alpha = jnp.exp(m - m_new)
        p = jnp.exp(s - m_new)
        l = alpha * l + jnp.sum(p, axis=1, keepdims=True)
        acc = alpha * acc + jnp.dot(p.astype(_BF16), v, preferred_element_type=_F32)
        return m_new, l, acc

    init = (jnp.full((tq, 1), -jnp.inf, _F32), jnp.zeros((tq, 1), _F32), jnp.zeros((tq, dh), _F32))
    carry = lax.fori_loop(0, qi, lambda kj, c: step(kj, c, False), init)
    _, l, acc = step(qi, carry, True)
    o_ref[...] = (acc / l).astype(o_ref.dtype)


def _fox_prompt(q, k, v, cum_tok, cum_heads, bsz, seq, heads, dh):
    tq = _tile(seq, 512, _LANES)
    nq = seq // tq
    return pl.pallas_call(
        functools.partial(_fox_prompt_body, tq=tq, scale=dh ** -0.5),
        grid=(bsz, heads, nq),
        in_specs=[pl.BlockSpec((tq, dh), lambda b, h, i: (b * nq + i, h)),
                  pl.BlockSpec((seq, dh), lambda b, h, i: (b, h)),
                  pl.BlockSpec((seq, dh), lambda b, h, i: (b, h)),
                  pl.BlockSpec((tq, cum_tok.shape[1]), lambda b, h, i: (b * nq + i, 0)),
                  pl.BlockSpec((None, heads, seq), lambda b, h, i: (b, 0, 0))],
        out_specs=pl.BlockSpec((tq, dh), lambda b, h, i: (b * nq + i, h)),
        out_shape=jax.ShapeDtypeStruct((bsz * seq, 2 * heads * dh), _BF16),
        scratch_shapes=[pltpu.VMEM((seq, dh), _BF16), pltpu.VMEM((seq, dh), _BF16)],
        compiler_params=_params("parallel", "parallel", "arbitrary"),
        name="fox_prompt",
    )(q, k, v, cum_tok, cum_heads)


def _split_heads_body(x_ref, o_ref, *, heads, dh):
    rows = x_ref.shape[0]
    for h in range(heads):
        o_ref[pl.ds(h, rows, stride=heads), :] = x_ref[:, h * dh:(h + 1) * dh]


def _split_heads(x, heads, dh):
    m = x.shape[0]
    tr = _tile(m, 512, _SUBLANES)
    return pl.pallas_call(
        functools.partial(_split_heads_body, heads=heads, dh=dh),
        grid=(m // tr,),
        in_specs=[pl.BlockSpec((tr, heads * dh), lambda i: (i, 0))],
        out_specs=pl.BlockSpec((tr * heads, dh), lambda i: (i, 0)),
        out_shape=jax.ShapeDtypeStruct((m * heads, dh), x.dtype),
        compiler_params=_params("parallel"),
        name="split_heads",
    )(x)


def _fox_sample_body(pt_ref, qbd_ref, kn_ref, vn_ref, gn_ref, *rest, heads, nq, dh, pps):
    del pt_ref
    kc_refs, vc_refs, gc_refs = rest[:pps], rest[pps:2 * pps], rest[2 * pps:3 * pps]
    o_ref, m_ref, l_ref, acc_ref, carry_ref, lq_ref, kb_ref, vb_ref = rest[3 * pps:]
    step = pl.program_id(1)
    hq = heads * nq
    page = kn_ref.shape[0]
    eye = (lax.broadcasted_iota(jnp.int32, (heads, heads), 0)
           == lax.broadcasted_iota(jnp.int32, (heads, heads), 1)).astype(_F32)
    key = lax.broadcasted_iota(jnp.int32, (heads, page), 1)

    def heads_by_keys(g_ref):
        return lax.dot_general(eye, g_ref[...], _NT, precision=_HI, preferred_element_type=_F32)

    def scan_keys(x, suffix):
        for b in range(page.bit_length() - 1):
            s = 1 << b
            if suffix:
                x = x + jnp.where(key < page - s, pltpu.roll(x, page - s, axis=1), 0.0)
            else:
                x = x + jnp.where(key >= s, pltpu.roll(x, s, axis=1), 0.0)
        return x

    def per_query_rows(x):
        return jnp.broadcast_to(x[:, None, :], (heads, nq, x.shape[1])).reshape(hq, x.shape[1])

    def attend(kb, vb, bias, valid):
        s = lax.dot_general(qbd_ref[...], kb, _NT, preferred_element_type=_F32) + bias
        if valid is not None:
            s = jnp.where(valid, s, -jnp.inf)
        m_prev = m_ref[...]
        m_new = jnp.maximum(m_prev, jnp.max(s, axis=1, keepdims=True))
        alpha = jnp.exp(m_prev - m_new)
        p = jnp.exp(s - m_new)
        l_ref[...] = alpha * l_ref[...] + jnp.sum(p, axis=1, keepdims=True)
        acc_ref[...] = alpha * acc_ref[...] + jnp.dot(p.astype(_BF16), vb, preferred_element_type=_F32)
        m_ref[...] = m_new

    @pl.when(step == 0)
    def _():
        lkeys = per_query_rows(scan_keys(heads_by_keys(gn_ref), suffix=False))
        qrow = lax.broadcasted_iota(jnp.int32, (hq, page), 0) % nq
        lane = lax.broadcasted_iota(jnp.int32, (hq, page), 1)
        lq = jnp.sum(jnp.where(lane == qrow, lkeys, 0.0), axis=1, keepdims=True)
        lq_ref[...] = lq
        carry_ref[...] = jnp.zeros_like(carry_ref)
        m_ref[...] = jnp.full_like(m_ref, -jnp.inf)
        l_ref[...] = jnp.zeros_like(l_ref)
        acc_ref[...] = jnp.zeros_like(acc_ref)
        attend(kn_ref[...].astype(_BF16), vn_ref[...].astype(_BF16), lq - lkeys, lane <= qrow)

    @pl.when(step > 0)
    def _():
        run = carry_ref[...]
        parts = [None] * pps
        for p in reversed(range(pps)):
            gt = heads_by_keys(gc_refs[p])
            incl = scan_keys(gt, suffix=True)
            parts[p] = incl - gt + run
            run = run + incl[:, 0:1]
            kb_ref[p * page:(p + 1) * page, :] = kc_refs[p][...].astype(_BF16)
            vb_ref[p * page:(p + 1) * page, :] = vc_refs[p][...].astype(_BF16)
        carry_ref[...] = run
        later = parts[0] if pps == 1 else jnp.concatenate(parts, axis=1)
        attend(kb_ref[...], vb_ref[...], per_query_rows(later) + lq_ref[...], None)

    @pl.when(step == pl.num_programs(1) - 1)
    def _():
        inv = 1.0 / l_ref[...]
        for h in range(heads):
            rows = slice(h * nq, (h + 1) * nq)
            cols = slice(h * dh, (h + 1) * dh)
            o_ref[:, cols] = (acc_ref[rows, cols] * inv[rows]).astype(o_ref.dtype)


def _fox_sample(qbd, knew, vnew, gnew, cache_k, cache_v, cache_logf, page_table, layer, nq, heads, dh):
    bsz, n_pages = page_table.shape
    ne, n_pool, page = cache_k.shape[:3]
    width = heads * dh
    hq = heads * nq
    ck = cache_k.reshape(ne, n_pool, page, width)
    cv = cache_v.reshape(ne, n_pool, page, width)
    pps = _tile(n_pages, 4, 1)
    nblk = n_pages // pps

    def new_map(b, s, pt):
        return (b, 0, 0)

    def page_map(p):
        def index(b, s, pt):
            return (layer, pt[b * n_pages + (nblk - jnp.maximum(s, 1)) * pps + p], 0, 0)
        return index

    wide = [pl.BlockSpec((None, None, page, width), page_map(p)) for p in range(pps)]
    narrow = [pl.BlockSpec((None, None, page, heads), page_map(p)) for p in range(pps)]
    grid_spec = pltpu.PrefetchScalarGridSpec(
        num_scalar_prefetch=1,
        grid=(bsz, nblk + 1),
        in_specs=[pl.BlockSpec((None, hq, width), new_map),
                  pl.BlockSpec((None, page, width), new_map),
                  pl.BlockSpec((None, page, width), new_map),
                  pl.BlockSpec((None, page, heads), new_map)] + wide + wide + narrow,
        out_specs=pl.BlockSpec((None, nq, width), new_map),
        scratch_shapes=[pltpu.VMEM((hq, 1), _F32), pltpu.VMEM((hq, 1), _F32),
                        pltpu.VMEM((hq, width), _F32), pltpu.VMEM((heads, 1), _F32),
                        pltpu.VMEM((hq, 1), _F32),
                        pltpu.VMEM((pps * page, width), _BF16), pltpu.VMEM((pps * page, width), _BF16)],
    )
    return pl.pallas_call(
        functools.partial(_fox_sample_body, heads=heads, nq=nq, dh=dh, pps=pps),
        grid_spec=grid_spec,
        out_shape=jax.ShapeDtypeStruct((bsz, nq, 2 * width), _BF16),
        compiler_params=_params("parallel", "arbitrary"),
        name="fox_sample",
    )(page_table.reshape(-1), qbd, knew, vnew, gnew, *([ck] * pps), *([cv] * pps), *([cache_logf] * pps))


def _gelu_tanh(y):
    return 0.5 * y * (1.0 + jnp.tanh(math.sqrt(2.0 / math.pi) * (y + 0.044715 * (y * y * y))))


def _tile_rows(x, n):
    return jnp.broadcast_to(x[None], (n,) + x.shape).reshape(n * x.shape[0], x.shape[1])


def _repeat_rows(x, n):
    return jnp.broadcast_to(x[:, None, :], (x.shape[0], n, x.shape[1])).reshape(x.shape[0] * n, x.shape[1])


def _s5_chunk_operators(bbr, bbi, cr, ci, crt, cit, prv, piv, prt, pit, glu, *, ch, t_len, ns):
    tc = t_len * ch
    per = _LANES // ch
    bbr_t, bbi_t = _tile_rows(bbr, t_len), _tile_rows(bbi, t_len)
    pr_r, pi_r = _repeat_rows(prv, ch), _repeat_rows(piv, ch)
    x_re = bbr_t * pr_r - bbi_t * pi_r
    x_im = bbr_t * pi_r + bbi_t * pr_r
    ws = jnp.concatenate([x_re, x_im], axis=1)
    k_rep = (lax.dot_general(x_re, _tile_rows(cr, per), _NT, precision=_HI, preferred_element_type=_F32)
             - lax.dot_general(x_im, _tile_rows(ci, per), _NT, precision=_HI, preferred_element_type=_F32))
    k_ext = jnp.concatenate([k_rep, jnp.zeros((tc, _LANES), _F32)], axis=0)
    lane_blk = lax.broadcasted_iota(jnp.int32, (tc, _LANES), 1) // ch
    cols = []
    for col in range(tc // _LANES):
        acc = jnp.zeros((tc, _LANES), _F32)
        for tt in range(per):
            r0 = ch * (t_len - 1 - (col * per + tt))
            acc = jnp.where(lane_blk == tt, k_ext[r0:r0 + tc, :], acc)
        cols.append(acc)
    wy = cols[0] if len(cols) == 1 else jnp.concatenate(cols, axis=1)
    e_mat = (lax.broadcasted_iota(jnp.int32, (t_len, tc), 0)
             == lax.broadcasted_iota(jnp.int32, (t_len, tc), 1) // ch).astype(_F32)
    f_mat = (lax.broadcasted_iota(jnp.int32, (ch, tc), 0)
             == lax.broadcasted_iota(jnp.int32, (ch, tc), 1) % ch).astype(_F32)
    hdot = functools.partial(jnp.dot, precision=_HI, preferred_element_type=_F32)
    pr1, pi1 = hdot(prt, e_mat), hdot(pit, e_mat)
    crx, cix = hdot(crt, f_mat), hdot(cit, f_mat)
    wh = jnp.concatenate([pr1 * crx - pi1 * cix, -(pi1 * crx + pr1 * cix)], axis=0)
    same_t = (lax.broadcasted_iota(jnp.int32, (tc, tc), 0) // ch
              == lax.broadcasted_iota(jnp.int32, (tc, tc), 1) // ch)
    wg = jnp.where(same_t, _tile_rows(hdot(glu, f_mat), t_len), 0.0)
    return wy, ws, wh, wg


def _s5_body(z_ref, bbr_ref, bbi_ref, cr_ref, ci_ref, crt_ref, cit_ref, prv_ref, piv_ref, prt_ref, pit_ref,
             glu_ref, dv_ref, gb_ref, a1_ref, a2_ref, h0_ref, mix_ref,
             o_ref, hf_ref, pin_ref, pout_ref, ybuf_ref, *, gpb, ch, t_len, bsz, nc, ns):
    del mix_ref
    tc = t_len * ch
    rows = bsz * nc
    per = _LANES // ch
    ncol = tc // _LANES
    for t in range(t_len):
        pin_ref[t] = z_ref[pl.ds(t, rows, stride=t_len), :]
    pout_ref[...] = jnp.zeros_like(pout_ref)
    slot = lax.broadcasted_iota(jnp.int32, (rows, _LANES), 1) // ch

    def one(g, _):
        wy, ws, wh, wg = _s5_chunk_operators(
            bbr_ref[g], bbi_ref[g], cr_ref[g], ci_ref[g], crt_ref[g], cit_ref[g], prv_ref[g], piv_ref[g],
            prt_ref[g], pit_ref[g], glu_ref[g], ch=ch, t_len=t_len, ns=ns)
        cols = []
        for col in range(ncol):
            acc = jnp.zeros((rows, _LANES), _F32)
            for tt in range(per):
                shift = ((tt - g + per) * ch) % _LANES
                acc = jnp.where(slot == tt, pltpu.roll(pin_ref[col * per + tt], shift, axis=1), acc)
            cols.append(acc)
        u = cols[0] if ncol == 1 else jnp.concatenate(cols, axis=1)
        ub = u.astype(_BF16)
        yin = jnp.dot(ub, wy.astype(_BF16), preferred_element_type=_F32)
        a1 = a1_ref[g]
        a2 = a2_ref[g]
        if nc == 1:
            st = jnp.dot(u, ws, precision=_HI, preferred_element_type=_F32)
            hstart = h0_ref[g]
            hf_ref[g] = a1[0:1] * hstart + a2[0:1] * pltpu.roll(hstart, ns, axis=1) + st
        else:
            x = jnp.dot(ub, ws.astype(_BF16), preferred_element_type=_F32)
            chunk = lax.broadcasted_iota(jnp.int32, (rows, 2 * ns), 0) % nc
            for k in range(nc.bit_length() - 1):
                sh = 1 << k
                xs = pltpu.roll(x, sh, axis=0)
                upd = a1[k:k + 1] * xs + a2[k:k + 1] * pltpu.roll(xs, ns, axis=1)
                x = x + jnp.where(chunk >= sh, upd, 0.0)
            hstart = jnp.where(chunk >= 1, pltpu.roll(x, 1, axis=0), 0.0)
            for b in range(bsz):
                hf_ref[g, b:b + 1, :] = x[(b + 1) * nc - 1:(b + 1) * nc, :]
        y = yin + jnp.dot(hstart.astype(_BF16), wh.astype(_BF16), preferred_element_type=_F32)
        y = _gelu_tanh(y + dv_ref[g] * u)
        gate = jax.nn.sigmoid(jnp.dot(y.astype(_BF16), wg.astype(_BF16),
                                      preferred_element_type=_F32) + gb_ref[g])
        out = y * gate
        for col in range(ncol):
            for tt in range(per):
                shift = ((g - tt + per) * ch) % _LANES
                piece = pltpu.roll(out[:, col * _LANES:(col + 1) * _LANES], shift, axis=1)
                t = col * per + tt
                pout_ref[t] = jnp.where(slot == g, piece, pout_ref[t])
        return 0

    lax.fori_loop(0, gpb, one, 0)
    for t in range(t_len):
        ybuf_ref[pl.ds(t, rows, stride=t_len), :] = pout_ref[t]
    o_ref[...] = ybuf_ref[...].astype(o_ref.dtype)


def _s5_operators(lam_re, lam_im, log_step, b_re, b_im, c_re, c_im, d, glu_w, glu_b, t_len, n_scan):
    groups, ns = lam_re.shape
    dt = jnp.exp(log_step)[:, None]

    def power(tau):
        tau = tau.astype(_F32)[None, :, None]
        mag = jnp.exp(lam_re[:, None, :] * dt[:, None, :] * tau)
        ang = lam_im[:, None, :] * dt[:, None, :] * tau
        return mag * jnp.cos(ang), mag * jnp.sin(ang)

    ab_re, ab_im = (p[:, 0] for p in power(jnp.ones((1,))))
    nr, ni = ab_re - 1.0, ab_im
    den = lam_re * lam_re + lam_im * lam_im
    f_re = (nr * lam_re + ni * lam_im) / den
    f_im = (ni * lam_re - nr * lam_im) / den
    bb_re = f_re[..., None] * b_re - f_im[..., None] * b_im
    bb_im = f_re[..., None] * b_im + f_im[..., None] * b_re
    pr, pi = power(jnp.arange(t_len + 1))
    rev = t_len - 1 - jnp.arange(t_len)
    dv = jnp.tile(d, (1, t_len))[:, None, :]
    gb = jnp.tile(glu_b, (1, t_len))[:, None, :]
    sr, si = power(t_len * (2 ** jnp.arange(_SUBLANES)))
    keep = (jnp.arange(_SUBLANES) < max(n_scan, 1))[None, :, None]
    sr, si = jnp.where(keep, sr, 0.0), jnp.where(keep, si, 0.0)
    a1 = jnp.concatenate([sr, sr], axis=-1)
    a2 = jnp.concatenate([-si, si], axis=-1)
    return (bb_re.transpose(0, 2, 1), bb_im.transpose(0, 2, 1), c_re, c_im,
            c_re.transpose(0, 2, 1), c_im.transpose(0, 2, 1), pr[:, rev], pi[:, rev],
            pr[:, 1:].transpose(0, 2, 1), pi[:, 1:].transpose(0, 2, 1), glu_w, dv, gb, a1, a2)


def _s5(u, h0, ops, mix, bsz, seq, t_len):
    groups, ch, ns = ops[0].shape
    ns2 = 2 * ns
    tc = t_len * ch
    nc = seq // t_len
    rows = bsz * nc
    total = bsz * seq
    assert tc % _LANES == 0 and _LANES % ch == 0
    gpb = _LANES // ch
    if h0 is None:
        h0 = jnp.zeros((groups, bsz, ns2), _F32)
    col0 = mix.shape[1] // 2 // _LANES

    def gmap(i):
        return (i, 0, 0)

    def per_group(a):
        return pl.BlockSpec((gpb,) + a.shape[1:], gmap)

    return pl.pallas_call(
        functools.partial(_s5_body, gpb=gpb, ch=ch, t_len=t_len, bsz=bsz, nc=nc, ns=ns),
        grid=(groups // gpb,),
        in_specs=([pl.BlockSpec((total, _LANES), lambda i: (0, i))]
                  + [per_group(a) for a in ops] + [per_group(h0), pl.BlockSpec(memory_space=pl.ANY)]),
        out_specs=(pl.BlockSpec((total, _LANES), lambda i: (0, col0 + i)),
                   pl.BlockSpec((gpb, bsz, ns2), gmap)),
        out_shape=(jax.ShapeDtypeStruct(mix.shape, mix.dtype),
                   jax.ShapeDtypeStruct((groups, bsz, ns2), _F32)),
        scratch_shapes=[pltpu.VMEM((t_len, rows, _LANES), _F32),
                        pltpu.VMEM((t_len, rows, _LANES), _F32),
                        pltpu.VMEM((total, _LANES), _F32)],
        input_output_aliases={len(ops) + 2: 0},
        compiler_params=_params("arbitrary"),
        name="s5_chunks",
    )(u, *ops, h0, mix)


def _conv_body(u_ref, halo_ref, st_ref, w_ref, wb_ref, lg_ref, lb_ref, o_ref, win_ref, sh_ref, y_ref,
               *, nblk, taps, rc, cc):
    i = pl.program_id(0)
    rows, d = u_ref.shape
    hb = halo_ref.shape[0]
    span = hb + rows
    first = (i % nblk) == 0

    @pl.when(first)
    def _():
        win_ref[0:hb, :] = st_ref[...]

    @pl.when(jnp.logical_not(first))
    def _():
        win_ref[0:hb, :] = halo_ref[...]

    win_ref[hb:span, :] = u_ref[...]
    win_ref[span:span + _SUBLANES, :] = jnp.zeros((_SUBLANES, d), _F32)
    off = hb - (taps - 1)

    def col_loop(ci, _):
        c0 = pl.multiple_of(ci * cc, cc)
        for r in range(_SUBLANES):
            sh_ref[r] = win_ref[pl.ds(r, span), pl.ds(c0, cc)]
        bias = wb_ref[:, pl.ds(c0, cc)]
        for r0 in range(0, rows, rc):
            acc = jnp.zeros((rc, cc), _F32) + bias
            for j in range(taps):
                a, r = divmod(off + j, _SUBLANES)
                acc = acc + sh_ref[r, pl.ds(r0 + _SUBLANES * a, rc), :] * w_ref[pl.ds(j, 1), pl.ds(c0, cc)]
            y_ref[r0:r0 + rc, pl.ds(c0, cc)] = acc
        return 0

    lax.fori_loop(0, d // cc, col_loop, 0)
    y = y_ref[...]
    mu = jnp.mean(y, axis=-1, keepdims=True)
    yc = y - mu
    var = jnp.mean(yc * yc, axis=-1, keepdims=True)
    z = yc * lax.rsqrt(var + LN_EPS) * lg_ref[...] + lb_ref[...]
    o_ref[...] = (z * jax.nn.sigmoid(z)).astype(o_ref.dtype)


def _conv_ln_silu(u, state, dw, dw_b, ln_g, ln_b, seq):
    rows_total, d = u.shape
    taps = dw.shape[0]
    hb = -(-(taps - 1) // _SUBLANES) * _SUBLANES
    st = jnp.pad(state.astype(_F32), ((0, 0), (hb - (taps - 1), 0), (0, 0)))
    rows = _tile(seq, 256, hb) if seq % hb == 0 else seq
    nblk = seq // rows
    rc = _tile(rows, 64, _SUBLANES)
    cc = _tile(d, 256, _LANES)
    per = rows // hb if nblk > 1 else 1
    halo_src = u if nblk > 1 else st[0]

    def halo_map(i):
        return (jnp.maximum(i * per - 1, 0) if nblk > 1 else 0, 0)

    vec = pl.BlockSpec((1, d), lambda i: (0, 0))
    return pl.pallas_call(
        functools.partial(_conv_body, nblk=nblk, taps=taps, rc=rc, cc=cc),
        grid=(rows_total // rows,),
        in_specs=[pl.BlockSpec((rows, d), lambda i: (i, 0)),
                  pl.BlockSpec((hb, d), halo_map),
                  pl.BlockSpec((None, hb, d), lambda i: (i // nblk, 0, 0)),
                  pl.BlockSpec((taps, d), lambda i: (0, 0)),
                  vec, vec, vec],
        out_specs=pl.BlockSpec((rows, d), lambda i: (i, 0)),
        out_shape=jax.ShapeDtypeStruct((rows_total, d), _BF16),
        scratch_shapes=[pltpu.VMEM((hb + rows + _SUBLANES, d), _F32),
                        pltpu.VMEM((_SUBLANES, hb + rows, cc), _F32),
                        pltpu.VMEM((rows, d), _F32)],
        compiler_params=_params("parallel"),
        name="conv_ln_silu",
    )(u, halo_src, st, dw, dw_b.reshape(1, d), ln_g.reshape(1, d), ln_b.reshape(1, d))


def kernel(x_prompt, x_sample, cache_k, cache_v, cache_logf, state_s5_re, state_s5_im, state_conv, page_table, c_prompt, c_sample, mod_w, mod_b, norm_mix_pre, norm_mix_post, norm_ffn_pre, norm_ffn_post, ffn_w_gate, ffn_w_up, ffn_w_down, hyb_w_in, hyb_b_f, hyb_w_out, s5_lambda_re, s5_lambda_im, s5_log_step, s5_b_re, s5_b_im, s5_c_re, s5_c_im, s5_d, s5_glu_w, s5_glu_b, conv_w_in, conv_b_in, conv_dw, conv_dw_b, conv_ln_g, conv_ln_b, conv_w_out):
    bp, lp, d = x_prompt.shape
    bs, ls, _ = x_sample.shape
    depth = mod_w.shape[0]
    heads, dh = cache_k.shape[3], cache_k.shape[4]
    fw = heads * dh
    ns = s5_lambda_re.shape[2]
    s5w = s5_b_re.shape[1] * s5_b_re.shape[-1]
    taps = conv_dw.shape[1]
    page = cache_k.shape[2]
    hidden = ffn_w_gate.shape[2]
    seqs = ((bp, lp), (bs, ls))

    n_c = bp + bs
    c_rows = -(-n_c // (2 * _SUBLANES)) * (2 * _SUBLANES)
    c_all = jnp.pad(jnp.concatenate([c_prompt, c_sample], axis=0), ((0, c_rows - n_c), (0, 0)))
    mod = _modulation(c_all, mod_w, mod_b).reshape(depth, c_rows, 6, d)

    def mod_terms(i, grp):
        lo, hi = (0, bp) if grp == 0 else (bp, bp + bs)
        return [mod[i, lo:hi, t][:, None, :] for t in range(6)]

    xs = [x_prompt.reshape(bp * lp, d), x_sample.reshape(bs * ls, d)]
    terms = [mod_terms(0, g) for g in range(2)]
    hs = [_norm_mod(xs[g], norm_mix_pre[0], terms[g][1], terms[g][0], seqs[g][1]) for g in range(2)]

    outs = {k: ([], []) for k in ("k", "v", "f", "s5r", "s5i", "conv")}
    conv_bias = conv_b_in.reshape(conv_b_in.shape[0], 1, conv_b_in.shape[1])

    for i in range(depth):
        j = i // 2
        if i % 2 == 0:
            w_u = hyb_w_in[j][:, 3 * fw + heads:][None]
            w_f = jnp.pad(hyb_w_in[j][:, 3 * fw:3 * fw + heads], ((0, 0), (0, _LANES - heads)))[None]
            b_f = jnp.pad(hyb_b_f[j], (0, _LANES - heads)).reshape(1, 1, _LANES)
            s5_raw = (s5_lambda_re[j], s5_lambda_im[j], s5_log_step[j], s5_b_re[j], s5_b_im[j],
                      s5_c_re[j], s5_c_im[j], s5_d[j], s5_glu_w[j], s5_glu_b[j])
            mixed = []
            for g, (bsz, seq) in enumerate(seqs):
                q, k, v = _mm(hs[g], [(hyb_w_in, j, 0)], [], 3 * fw, _ep_id, _F32, 1024, 512, "in_proj_qkv",
                              parts=3)
                uz = _mm(hs[g], [(w_u, 0, 0)], [], s5w, _ep_id, _F32, 1024, 512, "in_proj_s5")
                logf = _mm(hs[g], [(w_f, 0, 0)], [(b_f, 0, 0)], _LANES, _ep_logsigmoid, _F32, 1024, _LANES,
                           "forget_gate")
                outs["k"][g].append(_split_heads(k, heads, dh).reshape(bsz, seq, heads, dh))
                outs["v"][g].append(_split_heads(v, heads, dh).reshape(bsz, seq, heads, dh))
                outs["f"][g].append(logf[:, :heads].reshape(bsz, seq, heads))
                if g == 0:
                    cum = _cumsum_rows(logf, seq)
                    cum_heads = cum[:, :heads].reshape(bsz, seq, heads).transpose(0, 2, 1)
                    mix = _fox_prompt(q, k, v, cum, cum_heads, bsz, seq, heads, dh)
                    t_len = _S5_CHUNK if seq % _S5_CHUNK == 0 else seq
                    h0 = None
                else:
                    qs = q.reshape(bsz, seq, heads, dh) * (dh ** -0.5)
                    eye = jnp.eye(heads, dtype=_F32)
                    qbd = (qs.transpose(0, 2, 1, 3)[:, :, :, None, :] * eye[None, :, None, :, None])
                    qbd = qbd.reshape(bsz, heads * seq, fw).astype(_BF16)
                    pad = ((0, 0), (0, page - seq), (0, 0))
                    knew = jnp.pad(k.reshape(bsz, seq, fw), pad)
                    vnew = jnp.pad(v.reshape(bsz, seq, fw), pad)
                    gnew = jnp.pad(logf[:, :heads].reshape(bsz, seq, heads), pad)
                    mix = _fox_sample(qbd, knew, vnew, gnew, cache_k, cache_v, cache_logf, page_table,
                                      j, seq, heads, dh).reshape(bsz * seq, 2 * fw)
                    t_len = seq
                    h0 = jnp.concatenate([state_s5_re[j], state_s5_im[j]], axis=-1).transpose(1, 0, 2)
                n_scan = (seq // t_len).bit_length() - 1
                ops = _s5_operators(*s5_raw, t_len, n_scan)
                mix, hf = _s5(uz, h0, ops, mix, bsz, seq, t_len)
                hf = hf.transpose(1, 0, 2)
                outs["s5r"][g].append(hf[..., :ns])
                outs["s5i"][g].append(hf[..., ns:])
                mixed.append(_mm(mix, [(hyb_w_out, j, 0)], [], d, _ep_id, _F32, 1024, 512, "out_proj"))
        else:
            mixed = []
            for g, (bsz, seq) in enumerate(seqs):
                u = _mm(hs[g], [(conv_w_in, j, 0), (conv_w_in, j, d)], [(conv_bias, j, 0), (conv_bias, j, d)],
                        d, _ep_glu, _F32, 1024, 256, "conv_in_glu")
                state = jnp.zeros((bsz, taps - 1, d), _F32) if g == 0 else state_conv[j]
                u3 = u.reshape(bsz, seq, d)
                if seq >= taps - 1:
                    outs["conv"][g].append(u3[:, seq - (taps - 1):])
                else:
                    outs["conv"][g].append(jnp.concatenate([state[:, seq:], u3], axis=1))
                y = _conv_ln_silu(u, state, conv_dw[j], conv_dw_b[j], conv_ln_g[j], conv_ln_b[j], seq)
                mixed.append(_mm(y, [(conv_w_out, j, 0)], [], d, _ep_id, _F32, 1024, 512, "conv_out"))

        nxt_terms = [mod_terms(i + 1, g) for g in range(2)] if i + 1 < depth else None
        w_down = _cast_bf16(ffn_w_down, i)
        for g, (bsz, seq) in enumerate(seqs):
            t = terms[g]
            x1, h2 = _res_norm(xs[g], mixed[g], norm_mix_post[i], t[2], (norm_ffn_pre[i], t[4], t[3]), seq)
            a = _mm(h2, [(ffn_w_gate, i, 0), (ffn_w_up, i, 0)], [], hidden, _ep_swiglu, _BF16, 1024, 256,
                    "ffn_gate_up")
            f = _mm(a, [(w_down, 0, 0)], [], d, _ep_id, _F32, 512, 256, "ffn_down")
            nxt = None if nxt_terms is None else (norm_mix_pre[i + 1], nxt_terms[g][1], nxt_terms[g][0])
            xs[g], hs[g] = _res_norm(x1, f, norm_ffn_post[i], t[5], nxt, seq)
        terms = nxt_terms

    def stack(key, g):
        parts = outs[key][g]
        return parts[0][None] if len(parts) == 1 else jnp.stack(parts)

    return (xs[0].reshape(bp, lp, d), xs[1].reshape(bs, ls, d),
            stack("k", 0), stack("v", 0), stack("f", 0),
            stack("k", 1), stack("v", 1), stack("f", 1),
            stack("s5r", 0), stack("s5i", 0), stack("s5r", 1), stack("s5i", 1),
            stack("conv", 0), stack("conv", 1))
```

```python
import functools
import math

import jax
import jax.numpy as jnp
from jax import lax
from jax.experimental import pallas as pl
from jax.experimental.pallas import tpu as pltpu

_BF16 = jnp.bfloat16
_F32 = jnp.float32
_HI = lax.Precision.HIGHEST

RMS_EPS = 1e-6
LN_EPS = 1e-5

_V7X_VMEM_BYTES = 64 * 1024 * 1024
_LANES = 128
_SUBLANES = 8
_VMEM_LIMIT = _V7X_VMEM_BYTES - 8 * 1024 * 1024
_S5_CHUNK = 16
_NT = (((1,), (1,)), ((), ()))


def _params(*sem):
    return pltpu.CompilerParams(dimension_semantics=sem, vmem_limit_bytes=_VMEM_LIMIT)


def _tile(dim, pref, align):
    if dim <= pref:
        return dim
    t = (pref // align) * align
    while t >= align:
        if dim % t == 0:
            return t
        t -= align
    return dim


def _mm_body(*refs, n_w, n_b, epilogue):
    x_ref = refs[0]
    w_refs = refs[1:1 + n_w]
    b_refs = refs[1 + n_w:1 + n_w + n_b]
    o_ref = refs[-1]
    x = x_ref[...]
    accs = [jnp.dot(x, w[...].astype(_BF16), preferred_element_type=_F32) for w in w_refs]
    if n_b:
        accs = [a + b[...] for a, b in zip(accs, b_refs)]
    o_ref[...] = epilogue(*accs).astype(o_ref.dtype)


def _mm(x, ws, bs, n, epilogue, out_dtype, tm, tn, name, parts=1):
    m, k = x.shape
    tm = _tile(m, tm, 2 * _SUBLANES)
    tn = _tile(n // parts, tn, _LANES)
    per_part = n // parts // tn

    def spec(rows, layer, col0):
        assert col0 % tn == 0
        return pl.BlockSpec((None, rows, tn), lambda i, j: (layer, 0, col0 // tn + j))

    in_specs = ([pl.BlockSpec((tm, k), lambda i, j: (i, 0))]
                + [spec(k, layer, col0) for _, layer, col0 in ws]
                + [spec(1, layer, col0) for _, layer, col0 in bs])
    out = pl.pallas_call(
        functools.partial(_mm_body, n_w=len(ws), n_b=len(bs), epilogue=epilogue),
        grid=(m // tm, n // tn),
        in_specs=in_specs,
        out_specs=pl.BlockSpec((None, tm, tn), lambda i, j: (j // per_part, i, j % per_part)),
        out_shape=jax.ShapeDtypeStruct((parts, m, n // parts), out_dtype),
        compiler_params=_params("parallel", "arbitrary"),
        name=name,
    )(x, *[w for w, _, _ in ws], *[b for b, _, _ in bs])
    return out if parts > 1 else out[0]


def _cast_body(x_ref, o_ref):
    o_ref[...] = x_ref[...].astype(o_ref.dtype)


def _cast_bf16(w, layer):
    _, r, c = w.shape
    tr = _tile(r, 512, 2 * _SUBLANES)
    return pl.pallas_call(
        _cast_body,
        grid=(r // tr,),
        in_specs=[pl.BlockSpec((None, tr, c), lambda i: (layer, i, 0))],
        out_specs=pl.BlockSpec((None, tr, c), lambda i: (0, i, 0)),
        out_shape=jax.ShapeDtypeStruct((1, r, c), _BF16),
        compiler_params=_params("parallel"),
        name="cast_bf16",
    )(w)


def _ep_id(a):
    return a


def _ep_swiglu(g, u):
    return g * jax.nn.sigmoid(g) * u


def _ep_glu(a, b):
    return a * jax.nn.sigmoid(b)


def _ep_logsigmoid(a):
    return jnp.minimum(a, 0.0) - jnp.log(1.0 + jnp.exp(-jnp.abs(a)))


def _mod_body(c_ref, w_ref, b_ref, o_ref):
    c = c_ref[...]
    a = (c * jax.nn.sigmoid(c)).astype(_BF16)
    o_ref[...] = jnp.dot(a, w_ref[...].astype(_BF16), preferred_element_type=_F32) + b_ref[...]


def _modulation(c, mod_w, mod_b):
    depth, d, n = mod_w.shape
    rows = c.shape[0]
    tn = _tile(n, 512, _LANES)
    return pl.pallas_call(
        _mod_body,
        grid=(depth, n // tn),
        in_specs=[pl.BlockSpec((rows, d), lambda l, j: (0, 0)),
                  pl.BlockSpec((None, d, tn), lambda l, j: (l, 0, j)),
                  pl.BlockSpec((None, 1, tn), lambda l, j: (l, 0, j))],
        out_specs=pl.BlockSpec((None, rows, tn), lambda l, j: (l, 0, j)),
        out_shape=jax.ShapeDtypeStruct((depth, rows, n), _F32),
        compiler_params=_params("parallel", "arbitrary"),
        name="modulation",
    )(c, mod_w, mod_b.reshape(depth, 1, n))


def _rms(x, g):
    return x * lax.rsqrt(jnp.mean(x * x, axis=-1, keepdims=True) + RMS_EPS) * g


def _norm_mod_body(x_ref, g_ref, sc_ref, sh_ref, h_ref):
    y = _rms(x_ref[...], g_ref[...])
    h_ref[...] = (y * (1.0 + sc_ref[...]) + sh_ref[...]).astype(h_ref.dtype)


def _res_norm_body(x_ref, o_ref, gpost_ref, gate_ref, *rest, with_next):
    xn = x_ref[...] + gate_ref[...] * _rms(o_ref[...].astype(_F32), gpost_ref[...])
    if with_next:
        gpre_ref, sc_ref, sh_ref, xn_ref, h_ref = rest
        xn_ref[...] = xn
        h_ref[...] = (_rms(xn, gpre_ref[...]) * (1.0 + sc_ref[...]) + sh_ref[...]).astype(h_ref.dtype)
    else:
        (xn_ref,) = rest
        xn_ref[...] = xn


def _row_specs(rows, d, tr, seq):
    per = seq // tr
    row = pl.BlockSpec((tr, d), lambda i: (i, 0))
    vec = pl.BlockSpec((1, d), lambda i: (0, 0))
    bvec = pl.BlockSpec((None, 1, d), lambda i: (i // per, 0, 0))
    return row, vec, bvec


def _norm_mod(x, g, scale, shift, seq):
    rows, d = x.shape
    tr = _tile(seq, 256, _SUBLANES)
    row, vec, bvec = _row_specs(rows, d, tr, seq)
    return pl.pallas_call(
        _norm_mod_body,
        grid=(rows // tr,),
        in_specs=[row, vec, bvec, bvec],
        out_specs=row,
        out_shape=jax.ShapeDtypeStruct((rows, d), _BF16),
        compiler_params=_params("parallel"),
        name="norm_mod",
    )(x, g.reshape(1, d), scale, shift)


def _res_norm(x, o, gpost, gate, nxt, seq):
    rows, d = x.shape
    tr = _tile(seq, 256, _SUBLANES)
    row, vec, bvec = _row_specs(rows, d, tr, seq)
    xs = jax.ShapeDtypeStruct((rows, d), _F32)
    if nxt is None:
        return pl.pallas_call(
            functools.partial(_res_norm_body, with_next=False),
            grid=(rows // tr,),
            in_specs=[row, row, vec, bvec],
            out_specs=row,
            out_shape=xs,
            compiler_params=_params("parallel"),
            name="res_norm_last",
        )(x, o, gpost.reshape(1, d), gate), None
    g, scale, shift = nxt
    return pl.pallas_call(
        functools.partial(_res_norm_body, with_next=True),
        grid=(rows // tr,),
        in_specs=[row, row, vec, bvec, vec, bvec, bvec],
        out_specs=(row, row),
        out_shape=(xs, jax.ShapeDtypeStruct((rows, d), _BF16)),
        compiler_params=_params("parallel"),
        name="res_norm",
    )(x, o, gpost.reshape(1, d), gate, g.reshape(1, d), scale, shift)


def _cumsum_body(x_ref, o_ref, *, blk):
    seq = x_ref.shape[0]
    r = lax.broadcasted_iota(jnp.int32, (blk, blk), 0)
    c = lax.broadcasted_iota(jnp.int32, (blk, blk), 1)
    tri = (c <= r).astype(_F32)
    carry = jnp.zeros((1, x_ref.shape[1]), _F32)
    for i in range(seq // blk):
        y = jnp.dot(tri, x_ref[i * blk:(i + 1) * blk, :], precision=_HI,
                    preferred_element_type=_F32) + carry
        o_ref[i * blk:(i + 1) * blk, :] = y
        carry = y[blk - 1:blk, :]


def _cumsum_rows(x, seq):
    rows, n = x.shape
    blk = _tile(seq, 256, _SUBLANES)
    return pl.pallas_call(
        functools.partial(_cumsum_body, blk=blk),
        grid=(rows // seq,),
        in_specs=[pl.BlockSpec((seq, n), lambda b: (b, 0))],
        out_specs=pl.BlockSpec((seq, n), lambda b: (b, 0)),
        out_shape=jax.ShapeDtypeStruct((rows, n), _F32),
        compiler_params=_params("parallel"),
        name="logf_cumsum",
    )(x)


def _fox_prompt_body(q_ref, k_ref, v_ref, cq_ref, ck_ref, o_ref, kb_ref, vb_ref, *, tq, scale):
    h = pl.program_id(1)
    qi = pl.program_id(2)

    @pl.when(qi == 0)
    def _():
        kb_ref[...] = k_ref[...].astype(_BF16)
        vb_ref[...] = v_ref[...].astype(_BF16)

    dh = q_ref.shape[1]
    q = (q_ref[...] * scale).astype(_BF16)
    head_lane = lax.broadcasted_iota(jnp.int32, cq_ref.shape, 1) == h
    cq = jnp.sum(jnp.where(head_lane, cq_ref[...], 0.0), axis=1, keepdims=True)

    def step(kj, carry, masked):
        m, l, acc = carry
        k0 = pl.multiple_of(kj * tq, tq)
        k = kb_ref[pl.ds(k0, tq), :]
        v = vb_ref[pl.ds(k0, tq), :]
        s = lax.dot_general(q, k, _NT, preferred_element_type=_F32)
        s = s + (cq - ck_ref[pl.ds(h, 1), pl.ds(k0, tq)])
        if masked:
            row = lax.broadcasted_iota(jnp.int32, (tq, tq), 0)
            col = lax.broadcasted_iota(jnp.int32, (tq, tq), 1)
            s = jnp.where(col <= row, s, -jnp.inf)
        m_new = jnp.maximum(m, jnp.max(s, axis=1, keepdims=True))
        alpha = jnp.exp(m - m_new)
        p = jnp.exp(s - m_new)
        l = alpha * l + jnp.sum(p, axis=1, keepdims=True)
        acc = alpha * acc + jnp.dot(p.astype(_BF16), v, preferred_element_type=_F32)
        return m_new, l, acc

    init = (jnp.full((tq, 1), -jnp.inf, _F32), jnp.zeros((tq, 1), _F32), jnp.zeros((tq, dh), _F32))
    carry = lax.fori_loop(0, qi, lambda kj, c: step(kj, c, False), init)
    _, l, acc = step(qi, carry, True)
    o_ref[...] = (acc / l).astype(o_ref.dtype)


def _fox_prompt(q, k, v, cum_tok, cum_heads, bsz, seq, heads, dh):
    tq = _tile(seq, 512, _LANES)
    nq = seq // tq
    return pl.pallas_call(
        functools.partial(_fox_prompt_body, tq=tq, scale=dh ** -0.5),
        grid=(bsz, heads, nq),
        in_specs=[pl.BlockSpec((tq, dh), lambda b, h, i: (b * nq + i, h)),
                  pl.BlockSpec((seq, dh), lambda b, h, i: (b, h)),
                  pl.BlockSpec((seq, dh), lambda b, h, i: (b, h)),
                  pl.BlockSpec((tq, cum_tok.shape[1]), lambda b, h, i: (b * nq + i, 0)),
                  pl.BlockSpec((None, heads, seq), lambda b, h, i: (b, 0, 0))],
        out_specs=pl.BlockSpec((tq, dh), lambda b, h, i: (b * nq + i, h)),
        out_shape=jax.ShapeDtypeStruct((bsz * seq, 2 * heads * dh), _BF16),
        scratch_shapes=[pltpu.VMEM((seq, dh), _BF16), pltpu.VMEM((seq, dh), _BF16)],
        compiler_params=_params("parallel", "parallel", "arbitrary"),
        name="fox_prompt",
    )(q, k, v, cum_tok, cum_heads)


def _split_heads_body(x_ref, o_ref, *, heads, dh):
    rows = x_ref.shape[0]
    for h in range(heads):
        o_ref[pl.ds(h, rows, stride=heads), :] = x_ref[:, h * dh:(h + 1) * dh]


def _split_heads(x, heads, dh):
    m = x.shape[0]
    tr = _tile(m, 512, _SUBLANES)
    return pl.pallas_call(
        functools.partial(_split_heads_body, heads=heads, dh=dh),
        grid=(m // tr,),
        in_specs=[pl.BlockSpec((tr, heads * dh), lambda i: (i, 0))],
        out_specs=pl.BlockSpec((tr * heads, dh), lambda i: (i, 0)),
        out_shape=jax.ShapeDtypeStruct((m * heads, dh), x.dtype),
        compiler_params=_params("parallel"),
        name="split_heads",
    )(x)


def _fox_sample_body(pt_ref, q_ref, kn_ref, vn_ref, gn_ref, *rest, heads, nq, dh, pps):
    del pt_ref
    kc_refs, vc_refs, gc_refs = rest[:pps], rest[pps:2 * pps], rest[2 * pps:3 * pps]
    o_ref, m_ref, l_ref, acc_ref, carry_ref, lq_ref, mask_ref, s_ref = rest[3 * pps:]
    step = pl.program_id(1)
    hq = heads * nq
    grows = gc_refs[0].shape[0]
    lane1 = lax.broadcasted_iota(jnp.int32, (1, _LANES), 1)
    row_head = lax.broadcasted_iota(jnp.int32, (hq, _LANES), 0) // nq
    row_query = lax.broadcasted_iota(jnp.int32, (hq, _LANES), 0) % nq
    lane_head = lax.broadcasted_iota(jnp.int32, (hq, _LANES), 1) % heads
    lane_key = lax.broadcasted_iota(jnp.int32, (hq, _LANES), 1) // heads

    def update(s, vb):
        m_prev = m_ref[...]
        m_new = jnp.maximum(m_prev, jnp.max(s, axis=1, keepdims=True))
        alpha = jnp.exp(m_prev - m_new)
        p = jnp.exp(s - m_new)
        l_ref[...] = alpha * l_ref[...] + jnp.sum(p, axis=1, keepdims=True)
        acc_ref[...] = alpha * acc_ref[...] + jnp.dot(p.astype(_BF16), vb, preferred_element_type=_F32)
        m_ref[...] = m_new

    @pl.when(step == 0)
    def _():
        x = gn_ref[...]
        s_ = heads
        while s_ < _LANES:
            x = x + jnp.where(lane1 >= s_, pltpu.roll(x, s_, axis=1), 0.0)
            s_ *= 2
        own = lane_head == row_head
        lq = jnp.sum(jnp.where(own & (lane_key == row_query), x, 0.0), axis=1, keepdims=True)
        lq_ref[...] = lq
        carry_ref[...] = jnp.zeros_like(carry_ref)
        m_ref[...] = jnp.full_like(m_ref, -jnp.inf)
        l_ref[...] = jnp.zeros_like(l_ref)
        acc_ref[...] = jnp.zeros_like(acc_ref)
        neg = jnp.where(own, 0.0, -jnp.inf)
        for r in range(grows):
            mask_ref[:, r * _LANES:(r + 1) * _LANES] = neg
        s = lax.dot_general(q_ref[...], kn_ref[...].astype(_BF16), _NT, preferred_element_type=_F32)
        s = jnp.where(own & (lane_key <= row_query), s + (lq - x), -jnp.inf)
        update(s, vn_ref[...].astype(_BF16))

    @pl.when(step > 0)
    def _():
        row = lax.broadcasted_iota(jnp.int32, (grows, _LANES), 0)
        lane = lax.broadcasted_iota(jnp.int32, (grows, _LANES), 1)
        for p in reversed(range(pps)):
            g = gc_refs[p][...]
            within = g
            total = g
            s_ = heads
            while s_ < _LANES:
                within = within + jnp.where(lane < _LANES - s_, pltpu.roll(within, _LANES - s_, axis=1), 0.0)
                total = total + pltpu.roll(total, s_, axis=1)
                s_ *= 2
            below = total
            s_ = 1
            while s_ < grows:
                below = below + jnp.where(row < grows - s_, pltpu.roll(below, grows - s_, axis=0), 0.0)
                s_ *= 2
            carry = carry_ref[...]
            later = (within - g) + (below - total) + carry
            carry_ref[...] = carry + below[0:1, :]
            kb = kc_refs[p][...].astype(_BF16)
            vb = vc_refs[p][...].astype(_BF16)
            s_ref[...] = lax.dot_general(q_ref[...], kb, _NT, preferred_element_type=_F32)
            for r in range(grows):
                cols = slice(r * _LANES, (r + 1) * _LANES)
                s_ref[:, cols] = s_ref[:, cols] + later[r:r + 1, :]
            update(s_ref[...] + mask_ref[...] + lq_ref[...], vb)

    @pl.when(step == pl.num_programs(1) - 1)
    def _():
        out = acc_ref[...] / l_ref[...]
        for h in range(heads):
            o_ref[:, h * dh:(h + 1) * dh] = out[h * nq:(h + 1) * nq, :].astype(o_ref.dtype)


def _fox_sample(q, knew, vnew, gnew, cache_k, cache_v, cache_logf, page_table, layer, nq, heads, dh):
    bsz, n_pages = page_table.shape
    ne, n_pool, page = cache_k.shape[:3]
    width = heads * dh
    hq = heads * nq
    assert nq * heads == _LANES and _LANES % heads == 0
    kh = page * heads
    ck = cache_k.reshape(ne, n_pool, kh, dh)
    cv = cache_v.reshape(ne, n_pool, kh, dh)
    cg = cache_logf.reshape(ne, n_pool, kh // _LANES, _LANES)
    pps = _tile(n_pages, 4, 1)
    nblk = n_pages // pps

    def new_map(b, s, pt):
        return (b, 0, 0)

    def page_map(p):
        def index(b, s, pt):
            return (layer, pt[b * n_pages + (nblk - jnp.maximum(s, 1)) * pps + p], 0, 0)
        return index

    wide = [pl.BlockSpec((None, None, kh, dh), page_map(p)) for p in range(pps)]
    narrow = [pl.BlockSpec((None, None, kh // _LANES, _LANES), page_map(p)) for p in range(pps)]
    grid_spec = pltpu.PrefetchScalarGridSpec(
        num_scalar_prefetch=1,
        grid=(bsz, nblk + 1),
        in_specs=[pl.BlockSpec((None, hq, dh), new_map),
                  pl.BlockSpec((hq, dh), lambda b, s, pt: (b, 0)),
                  pl.BlockSpec((hq, dh), lambda b, s, pt: (b, 0)),
                  pl.BlockSpec((None, 1, _LANES), new_map)] + wide + wide + narrow,
        out_specs=pl.BlockSpec((None, nq, width), new_map),
        scratch_shapes=[pltpu.VMEM((hq, 1), _F32), pltpu.VMEM((hq, 1), _F32),
                        pltpu.VMEM((hq, dh), _F32), pltpu.VMEM((1, _LANES), _F32),
                        pltpu.VMEM((hq, 1), _F32),
                        pltpu.VMEM((hq, kh), _F32), pltpu.VMEM((hq, kh), _F32)],
    )
    return pl.pallas_call(
        functools.partial(_fox_sample_body, heads=heads, nq=nq, dh=dh, pps=pps),
        grid_spec=grid_spec,
        out_shape=jax.ShapeDtypeStruct((bsz, nq, 2 * width), _BF16),
        compiler_params=_params("parallel", "arbitrary"),
        name="fox_sample",
    )(page_table.reshape(-1), q, knew, vnew, gnew, *([ck] * pps), *([cv] * pps), *([cg] * pps))


def _gelu_tanh(y):
    return 0.5 * y * (1.0 + jnp.tanh(math.sqrt(2.0 / math.pi) * (y + 0.044715 * (y * y * y))))


def _tile_rows(x, n):
    return jnp.broadcast_to(x[None], (n,) + x.shape).reshape(n * x.shape[0], x.shape[1])


def _repeat_rows(x, n):
    return jnp.broadcast_to(x[:, None, :], (x.shape[0], n, x.shape[1])).reshape(x.shape[0] * n, x.shape[1])


def _s5_chunk_operators(bbr, bbi, cr, ci, crt, cit, prv, piv, prt, pit, glu, *, ch, t_len, ns):
    tc = t_len * ch
    per = _LANES // ch
    bbr_t, bbi_t = _tile_rows(bbr, t_len), _tile_rows(bbi, t_len)
    pr_r, pi_r = _repeat_rows(prv, ch), _repeat_rows(piv, ch)
    x_re = bbr_t * pr_r - bbi_t * pi_r
    x_im = bbr_t * pi_r + bbi_t * pr_r
    ws = jnp.concatenate([x_re, x_im], axis=1)
    k_rep = (lax.dot_general(x_re, _tile_rows(cr, per), _NT, precision=_HI, preferred_element_type=_F32)
             - lax.dot_general(x_im, _tile_rows(ci, per), _NT, precision=_HI, preferred_element_type=_F32))
    k_ext = jnp.concatenate([k_rep, jnp.zeros((tc, _LANES), _F32)], axis=0)
    lane_blk = lax.broadcasted_iota(jnp.int32, (tc, _LANES), 1) // ch
    cols = []
    for col in range(tc // _LANES):
        acc = jnp.zeros((tc, _LANES), _F32)
        for tt in range(per):
            r0 = ch * (t_len - 1 - (col * per + tt))
            acc = jnp.where(lane_blk == tt, k_ext[r0:r0 + tc, :], acc)
        cols.append(acc)
    wy = cols[0] if len(cols) == 1 else jnp.concatenate(cols, axis=1)
    e_mat = (lax.broadcasted_iota(jnp.int32, (t_len, tc), 0)
             == lax.broadcasted_iota(jnp.int32, (t_len, tc), 1) // ch).astype(_F32)
    f_mat = (lax.broadcasted_iota(jnp.int32, (ch, tc), 0)
             == lax.broadcasted_iota(jnp.int32, (ch, tc), 1) % ch).astype(_F32)
    hdot = functools.partial(jnp.dot, precision=_HI, preferred_element_type=_F32)
    pr1, pi1 = hdot(prt, e_mat), hdot(pit, e_mat)
    crx, cix = hdot(crt, f_mat), hdot(cit, f_mat)
    wh = jnp.concatenate([pr1 * crx - pi1 * cix, -(pi1 * crx + pr1 * cix)], axis=0)
    same_t = (lax.broadcasted_iota(jnp.int32, (tc, tc), 0) // ch
              == lax.broadcasted_iota(jnp.int32, (tc, tc), 1) // ch)
    wg = jnp.where(same_t, _tile_rows(hdot(glu, f_mat), t_len), 0.0)
    return wy, ws, wh, wg


def _s5_body(z_ref, bbr_ref, bbi_ref, cr_ref, ci_ref, crt_ref, cit_ref, prv_ref, piv_ref, prt_ref, pit_ref,
             glu_ref, dv_ref, gb_ref, a1_ref, a2_ref, h0_ref, mix_ref,
             o_ref, hf_ref, pin_ref, pout_ref, ybuf_ref, *, gpb, ch, t_len, bsz, nc, ns):
    del mix_ref
    tc = t_len * ch
    rows = bsz * nc
    per = _LANES // ch
    ncol = tc // _LANES
    for t in range(t_len):
        pin_ref[t] = z_ref[pl.ds(t, rows, stride=t_len), :]
    pout_ref[...] = jnp.zeros_like(pout_ref)
    slot = lax.broadcasted_iota(jnp.int32, (rows, _LANES), 1) // ch

    def one(g, _):
        wy, ws, wh, wg = _s5_chunk_operators(
            bbr_ref[g], bbi_ref[g], cr_ref[g], ci_ref[g], crt_ref[g], cit_ref[g], prv_ref[g], piv_ref[g],
            prt_ref[g], pit_ref[g], glu_ref[g], ch=ch, t_len=t_len, ns=ns)
        cols = []
        for col in range(ncol):
            acc = jnp.zeros((rows, _LANES), _F32)
            for tt in range(per):
                shift = ((tt - g + per) * ch) % _LANES
                acc = jnp.where(slot == tt, pltpu.roll(pin_ref[col * per + tt], shift, axis=1), acc)
            cols.append(acc)
        u = cols[0] if ncol == 1 else jnp.concatenate(cols, axis=1)
        ub = u.astype(_BF16)
        yin = jnp.dot(ub, wy.astype(_BF16), preferred_element_type=_F32)
        a1 = a1_ref[g]
        a2 = a2_ref[g]
        if nc == 1:
            st = jnp.dot(u, ws, precision=_HI, preferred_element_type=_F32)
            hstart = h0_ref[g]
            hf_ref[g] = a1[0:1] * hstart + a2[0:1] * pltpu.roll(hstart, ns, axis=1) + st
        else:
            x = jnp.dot(ub, ws.astype(_BF16), preferred_element_type=_F32)
            chunk = lax.broadcasted_iota(jnp.int32, (rows, 2 * ns), 0) % nc
            for k in range(nc.bit_length() - 1):
                sh = 1 << k
                xs = pltpu.roll(x, sh, axis=0)
                upd = a1[k:k + 1] * xs + a2[k:k + 1] * pltpu.roll(xs, ns, axis=1)
                x = x + jnp.where(chunk >= sh, upd, 0.0)
            hstart = jnp.where(chunk >= 1, pltpu.roll(x, 1, axis=0), 0.0)
            for b in range(bsz):
                hf_ref[g, b:b + 1, :] = x[(b + 1) * nc - 1:(b + 1) * nc, :]
        y = yin + jnp.dot(hstart.astype(_BF16), wh.astype(_BF16), preferred_element_type=_F32)
        y = _gelu_tanh(y + dv_ref[g] * u)
        gate = jax.nn.sigmoid(jnp.dot(y.astype(_BF16), wg.astype(_BF16),
                                      preferred_element_type=_F32) + gb_ref[g])
        out = y * gate
        for col in range(ncol):
            for tt in range(per):
                shift = ((g - tt + per) * ch) % _LANES
                piece = pltpu.roll(out[:, col * _LANES:(col + 1) * _LANES], shift, axis=1)
                t = col * per + tt
                pout_ref[t] = jnp.where(slot == g, piece, pout_ref[t])
        return 0

    lax.fori_loop(0, gpb, one, 0)
    for t in range(t_len):
        ybuf_ref[pl.ds(t, rows, stride=t_len), :] = pout_ref[t]
    o_ref[...] = ybuf_ref[...].astype(o_ref.dtype)


def _s5_operators(lam_re, lam_im, log_step, b_re, b_im, c_re, c_im, d, glu_w, glu_b, t_len, n_scan):
    groups, ns = lam_re.shape
    dt = jnp.exp(log_step)[:, None]

    def power(tau):
        tau = tau.astype(_F32)[None, :, None]
        mag = jnp.exp(lam_re[:, None, :] * dt[:, None, :] * tau)
        ang = lam_im[:, None, :] * dt[:, None, :] * tau
        return mag * jnp.cos(ang), mag * jnp.sin(ang)

    ab_re, ab_im = (p[:, 0] for p in power(jnp.ones((1,))))
    nr, ni = ab_re - 1.0, ab_im
    den = lam_re * lam_re + lam_im * lam_im
    f_re = (nr * lam_re + ni * lam_im) / den
    f_im = (ni * lam_re - nr * lam_im) / den
    bb_re = f_re[..., None] * b_re - f_im[..., None] * b_im
    bb_im = f_re[..., None] * b_im + f_im[..., None] * b_re
    pr, pi = power(jnp.arange(t_len + 1))
    rev = t_len - 1 - jnp.arange(t_len)
    dv = jnp.tile(d, (1, t_len))[:, None, :]
    gb = jnp.tile(glu_b, (1, t_len))[:, None, :]
    sr, si = power(t_len * (2 ** jnp.arange(_SUBLANES)))
    keep = (jnp.arange(_SUBLANES) < max(n_scan, 1))[None, :, None]
    sr, si = jnp.where(keep, sr, 0.0), jnp.where(keep, si, 0.0)
    a1 = jnp.concatenate([sr, sr], axis=-1)
    a2 = jnp.concatenate([-si, si], axis=-1)
    return (bb_re.transpose(0, 2, 1), bb_im.transpose(0, 2, 1), c_re, c_im,
            c_re.transpose(0, 2, 1), c_im.transpose(0, 2, 1), pr[:, rev], pi[:, rev],
            pr[:, 1:].transpose(0, 2, 1), pi[:, 1:].transpose(0, 2, 1), glu_w, dv, gb, a1, a2)


def _s5(u, h0, ops, mix, bsz, seq, t_len):
    groups, ch, ns = ops[0].shape
    ns2 = 2 * ns
    tc = t_len * ch
    nc = seq // t_len
    rows = bsz * nc
    total = bsz * seq
    assert tc % _LANES == 0 and _LANES % ch == 0
    gpb = _LANES // ch
    if h0 is None:
        h0 = jnp.zeros((groups, bsz, ns2), _F32)
    col0 = mix.shape[1] // 2 // _LANES

    def gmap(i):
        return (i, 0, 0)

    def per_group(a):
        return pl.BlockSpec((gpb,) + a.shape[1:], gmap)

    return pl.pallas_call(
        functools.partial(_s5_body, gpb=gpb, ch=ch, t_len=t_len, bsz=bsz, nc=nc, ns=ns),
        grid=(groups // gpb,),
        in_specs=([pl.BlockSpec((total, _LANES), lambda i: (0, i))]
                  + [per_group(a) for a in ops] + [per_group(h0), pl.BlockSpec(memory_space=pl.ANY)]),
        out_specs=(pl.BlockSpec((total, _LANES), lambda i: (0, col0 + i)),
                   pl.BlockSpec((gpb, bsz, ns2), gmap)),
        out_shape=(jax.ShapeDtypeStruct(mix.shape, mix.dtype),
                   jax.ShapeDtypeStruct((groups, bsz, ns2), _F32)),
        scratch_shapes=[pltpu.VMEM((t_len, rows, _LANES), _F32),
                        pltpu.VMEM((t_len, rows, _LANES), _F32),
                        pltpu.VMEM((total, _LANES), _F32)],
        input_output_aliases={len(ops) + 2: 0},
        compiler_params=_params("arbitrary"),
        name="s5_chunks",
    )(u, *ops, h0, mix)


def _conv_body(u_ref, halo_ref, st_ref, w_ref, wb_ref, lg_ref, lb_ref, o_ref, win_ref, sh_ref, y_ref,
               *, nblk, taps, rc, cc):
    i = pl.program_id(0)
    rows, d = u_ref.shape
    hb = halo_ref.shape[0]
    span = hb + rows
    first = (i % nblk) == 0

    @pl.when(first)
    def _():
        win_ref[0:hb, :] = st_ref[...]

    @pl.when(jnp.logical_not(first))
    def _():
        win_ref[0:hb, :] = halo_ref[...]

    win_ref[hb:span, :] = u_ref[...]
    win_ref[span:span + _SUBLANES, :] = jnp.zeros((_SUBLANES, d), _F32)
    off = hb - (taps - 1)

    def col_loop(ci, _):
        c0 = pl.multiple_of(ci * cc, cc)
        for r in range(_SUBLANES):
            sh_ref[r] = win_ref[pl.ds(r, span), pl.ds(c0, cc)]
        bias = wb_ref[:, pl.ds(c0, cc)]
        for r0 in range(0, rows, rc):
            acc = jnp.zeros((rc, cc), _F32) + bias
            for j in range(taps):
                a, r = divmod(off + j, _SUBLANES)
                acc = acc + sh_ref[r, pl.ds(r0 + _SUBLANES * a, rc), :] * w_ref[pl.ds(j, 1), pl.ds(c0, cc)]
            y_ref[r0:r0 + rc, pl.ds(c0, cc)] = acc
        return 0

    lax.fori_loop(0, d // cc, col_loop, 0)
    y = y_ref[...]
    mu = jnp.mean(y, axis=-1, keepdims=True)
    yc = y - mu
    var = jnp.mean(yc * yc, axis=-1, keepdims=True)
    z = yc * lax.rsqrt(var + LN_EPS) * lg_ref[...] + lb_ref[...]
    o_ref[...] = (z * jax.nn.sigmoid(z)).astype(o_ref.dtype)


def _conv_ln_silu(u, state, dw, dw_b, ln_g, ln_b, seq):
    rows_total, d = u.shape
    taps = dw.shape[0]
    hb = -(-(taps - 1) // _SUBLANES) * _SUBLANES
    st = jnp.pad(state.astype(_F32), ((0, 0), (hb - (taps - 1), 0), (0, 0)))
    rows = _tile(seq, 256, hb) if seq % hb == 0 else seq
    nblk = seq // rows
    rc = _tile(rows, 64, _SUBLANES)
    cc = _tile(d, 256, _LANES)
    per = rows // hb if nblk > 1 else 1
    halo_src = u if nblk > 1 else st[0]

    def halo_map(i):
        return (jnp.maximum(i * per - 1, 0) if nblk > 1 else 0, 0)

    vec = pl.BlockSpec((1, d), lambda i: (0, 0))
    return pl.pallas_call(
        functools.partial(_conv_body, nblk=nblk, taps=taps, rc=rc, cc=cc),
        grid=(rows_total // rows,),
        in_specs=[pl.BlockSpec((rows, d), lambda i: (i, 0)),
                  pl.BlockSpec((hb, d), halo_map),
                  pl.BlockSpec((None, hb, d), lambda i: (i // nblk, 0, 0)),
                  pl.BlockSpec((taps, d), lambda i: (0, 0)),
                  vec, vec, vec],
        out_specs=pl.BlockSpec((rows, d), lambda i: (i, 0)),
        out_shape=jax.ShapeDtypeStruct((rows_total, d), _BF16),
        scratch_shapes=[pltpu.VMEM((hb + rows + _SUBLANES, d), _F32),
                        pltpu.VMEM((_SUBLANES, hb + rows, cc), _F32),
                        pltpu.VMEM((rows, d), _F32)],
        compiler_params=_params("parallel"),
        name="conv_ln_silu",
    )(u, halo_src, st, dw, dw_b.reshape(1, d), ln_g.reshape(1, d), ln_b.reshape(1, d))


def kernel(x_prompt, x_sample, cache_k, cache_v, cache_logf, state_s5_re, state_s5_im, state_conv, page_table, c_prompt, c_sample, mod_w, mod_b, norm_mix_pre, norm_mix_post, norm_ffn_pre, norm_ffn_post, ffn_w_gate, ffn_w_up, ffn_w_down, hyb_w_in, hyb_b_f, hyb_w_out, s5_lambda_re, s5_lambda_im, s5_log_step, s5_b_re, s5_b_im, s5_c_re, s5_c_im, s5_d, s5_glu_w, s5_glu_b, conv_w_in, conv_b_in, conv_dw, conv_dw_b, conv_ln_g, conv_ln_b, conv_w_out):
    bp, lp, d = x_prompt.shape
    bs, ls, _ = x_sample.shape
    depth = mod_w.shape[0]
    heads, dh = cache_k.shape[3], cache_k.shape[4]
    fw = heads * dh
    ns = s5_lambda_re.shape[2]
    s5w = s5_b_re.shape[1] * s5_b_re.shape[-1]
    taps = conv_dw.shape[1]
    hidden = ffn_w_gate.shape[2]
    seqs = ((bp, lp), (bs, ls))

    n_c = bp + bs
    c_rows = -(-n_c // (2 * _SUBLANES)) * (2 * _SUBLANES)
    c_all = jnp.pad(jnp.concatenate([c_prompt, c_sample], axis=0), ((0, c_rows - n_c), (0, 0)))
    mod = _modulation(c_all, mod_w, mod_b).reshape(depth, c_rows, 6, d)

    def mod_terms(i, grp):
        lo, hi = (0, bp) if grp == 0 else (bp, bp + bs)
        return [mod[i, lo:hi, t][:, None, :] for t in range(6)]

    xs = [x_prompt.reshape(bp * lp, d), x_sample.reshape(bs * ls, d)]
    terms = [mod_terms(0, g) for g in range(2)]
    hs = [_norm_mod(xs[g], norm_mix_pre[0], terms[g][1], terms[g][0], seqs[g][1]) for g in range(2)]

    outs = {k: ([], []) for k in ("k", "v", "f", "s5r", "s5i", "conv")}
    conv_bias = conv_b_in.reshape(conv_b_in.shape[0], 1, conv_b_in.shape[1])

    for i in range(depth):
        j = i // 2
        if i % 2 == 0:
            w_u = hyb_w_in[j][:, 3 * fw + heads:][None]
            w_f = jnp.pad(hyb_w_in[j][:, 3 * fw:3 * fw + heads], ((0, 0), (0, _LANES - heads)))[None]
            b_f = jnp.pad(hyb_b_f[j], (0, _LANES - heads)).reshape(1, 1, _LANES)
            s5_raw = (s5_lambda_re[j], s5_lambda_im[j], s5_log_step[j], s5_b_re[j], s5_b_im[j],
                      s5_c_re[j], s5_c_im[j], s5_d[j], s5_glu_w[j], s5_glu_b[j])
            mixed = []
            for g, (bsz, seq) in enumerate(seqs):
                q, k, v = _mm(hs[g], [(hyb_w_in, j, 0)], [], 3 * fw, _ep_id, _F32, 1024, 512, "in_proj_qkv",
                              parts=3)
                uz = _mm(hs[g], [(w_u, 0, 0)], [], s5w, _ep_id, _F32, 1024, 512, "in_proj_s5")
                logf = _mm(hs[g], [(w_f, 0, 0)], [(b_f, 0, 0)], _LANES, _ep_logsigmoid, _F32, 1024, _LANES,
                           "forget_gate")
                k_out, v_out = _split_heads(k, heads, dh), _split_heads(v, heads, dh)
                outs["k"][g].append(k_out.reshape(bsz, seq, heads, dh))
                outs["v"][g].append(v_out.reshape(bsz, seq, heads, dh))
                outs["f"][g].append(logf[:, :heads].reshape(bsz, seq, heads))
                if g == 0:
                    cum = _cumsum_rows(logf, seq)
                    cum_heads = cum[:, :heads].reshape(bsz, seq, heads).transpose(0, 2, 1)
                    mix = _fox_prompt(q, k, v, cum, cum_heads, bsz, seq, heads, dh)
                    t_len = _S5_CHUNK if seq % _S5_CHUNK == 0 else seq
                    h0 = None
                else:
                    qs = (q.reshape(bsz, seq, heads, dh) * (dh ** -0.5)).transpose(0, 2, 1, 3)
                    qs = qs.reshape(bsz, heads * seq, dh).astype(_BF16)
                    gnew = logf[:, :heads].reshape(bsz, 1, seq * heads)
                    mix = _fox_sample(qs, k_out, v_out, gnew, cache_k, cache_v, cache_logf, page_table,
                                      j, seq, heads, dh).reshape(bsz * seq, 2 * fw)
                    t_len = seq
                    h0 = jnp.concatenate([state_s5_re[j], state_s5_im[j]], axis=-1).transpose(1, 0, 2)
                n_scan = (seq // t_len).bit_length() - 1
                ops = _s5_operators(*s5_raw, t_len, n_scan)
                mix, hf = _s5(uz, h0, ops, mix, bsz, seq, t_len)
                hf = hf.transpose(1, 0, 2)
                outs["s5r"][g].append(hf[..., :ns])
                outs["s5i"][g].append(hf[..., ns:])
                mixed.append(_mm(mix, [(hyb_w_out, j, 0)], [], d, _ep_id, _F32, 1024, 512, "out_proj"))
        else:
            mixed = []
            for g, (bsz, seq) in enumerate(seqs):
                u = _mm(hs[g], [(conv_w_in, j, 0), (conv_w_in, j, d)], [(conv_bias, j, 0), (conv_bias, j, d)],
                        d, _ep_glu, _F32, 1024, 256, "conv_in_glu")
                state = jnp.zeros((bsz, taps - 1, d), _F32) if g == 0 else state_conv[j]
                u3 = u.reshape(bsz, seq, d)
                if seq >= taps - 1:
                    outs["conv"][g].append(u3[:, seq - (taps - 1):])
                else:
                    outs["conv"][g].append(jnp.concatenate([state[:, seq:], u3], axis=1))
                y = _conv_ln_silu(u, state, conv_dw[j], conv_dw_b[j], conv_ln_g[j], conv_ln_b[j], seq)
                mixed.append(_mm(y, [(conv_w_out, j, 0)], [], d, _ep_id, _F32, 1024, 512, "conv_out"))

        nxt_terms = [mod_terms(i + 1, g) for g in range(2)] if i + 1 < depth else None
        w_down = _cast_bf16(ffn_w_down, i)
        for g, (bsz, seq) in enumerate(seqs):
            t = terms[g]
            x1, h2 = _res_norm(xs[g], mixed[g], norm_mix_post[i], t[2], (norm_ffn_pre[i], t[4], t[3]), seq)
            a = _mm(h2, [(ffn_w_gate, i, 0), (ffn_w_up, i, 0)], [], hidden, _ep_swiglu, _BF16, 1024, 256,
                    "ffn_gate_up")
            f = _mm(a, [(w_down, 0, 0)], [], d, _ep_id, _F32, 512, 256, "ffn_down")
            nxt = None if nxt_terms is None else (norm_mix_pre[i + 1], nxt_terms[g][1], nxt_terms[g][0])
            xs[g], hs[g] = _res_norm(x1, f, norm_ffn_post[i], t[5], nxt, seq)
        terms = nxt_terms

    def stack(key, g):
        parts = outs[key][g]
        return parts[0][None] if len(parts) == 1 else jnp.stack(parts)

    return (xs[0].reshape(bp, lp, d), xs[1].reshape(bs, ls, d),
            stack("k", 0), stack("v", 0), stack("f", 0),
            stack("k", 1), stack("v", 1), stack("f", 1),
            stack("s5r", 0), stack("s5i", 0), stack("s5r", 1), stack("s5i", 1),
            stack("conv", 0), stack("conv", 1))
```

```python
import functools
import math

import jax
import jax.numpy as jnp
from jax import lax
from jax.experimental import pallas as pl
from jax.experimental.pallas import tpu as pltpu

_BF16 = jnp.bfloat16
_F32 = jnp.float32
_HI = lax.Precision.HIGHEST

RMS_EPS = 1e-6
LN_EPS = 1e-5

_V7X_VMEM_BYTES = 64 * 1024 * 1024
_LANES = 128
_SUBLANES = 8
_VMEM_LIMIT = _V7X_VMEM_BYTES - 8 * 1024 * 1024
_S5_CHUNK = 16
_NT = (((1,), (1,)), ((), ()))


def _params(*sem):
    return pltpu.CompilerParams(dimension_semantics=sem, vmem_limit_bytes=_VMEM_LIMIT)


def _tile(dim, pref, align):
    if dim <= pref:
        return dim
    t = (pref // align) * align
    while t >= align:
        if dim % t == 0:
            return t
        t -= align
    return dim


def _mm_body(*refs, n_x, n_w, n_b, epilogue):
    x_refs = refs[:n_x]
    w_refs = refs[n_x:n_x + n_x * n_w]
    b_refs = refs[n_x + n_x * n_w:n_x + n_x * n_w + n_b]
    o_ref = refs[-1]
    xs = [x[...] for x in x_refs]
    accs = []
    for i in range(n_w):
        parts = [jnp.dot(x, w[...].astype(_BF16), preferred_element_type=_F32)
                 for x, w in zip(xs, w_refs[i * n_x:(i + 1) * n_x])]
        accs.append(functools.reduce(lambda a, b: a + b, parts))
    if n_b:
        accs = [a + b[...] for a, b in zip(accs, b_refs)]
    o_ref[...] = epilogue(*accs).astype(o_ref.dtype)


def _mm(xs, ws, bs, n, epilogue, out_dtype, tm, tn, name, parts=1):
    m, k = xs[0].shape
    assert all(x.shape == (m, k) for x in xs)
    tm = _tile(m, tm, 2 * _SUBLANES)
    tn = _tile(n // parts, tn, _LANES)
    per_part = n // parts // tn

    def spec(rows, row_blk, layer, col0):
        assert col0 % tn == 0
        return pl.BlockSpec((None, rows, tn), lambda i, j: (layer, row_blk, col0 // tn + j))

    in_specs = ([pl.BlockSpec((tm, k), lambda i, j: (i, 0))] * len(xs)
                + [spec(k, r, layer, col0) for _, layer, col0 in ws for r in range(len(xs))]
                + [spec(1, 0, layer, col0) for _, layer, col0 in bs])
    out = pl.pallas_call(
        functools.partial(_mm_body, n_x=len(xs), n_w=len(ws), n_b=len(bs), epilogue=epilogue),
        grid=(m // tm, n // tn),
        in_specs=in_specs,
        out_specs=pl.BlockSpec((None, tm, tn), lambda i, j: (j // per_part, i, j % per_part)),
        out_shape=jax.ShapeDtypeStruct((parts, m, n // parts), out_dtype),
        compiler_params=_params("parallel", "arbitrary"),
        name=name,
    )(*xs, *[w for w, _, _ in ws for _ in xs], *[b for b, _, _ in bs])
    return out if parts > 1 else out[0]


def _cast_body(x_ref, o_ref):
    o_ref[...] = x_ref[...].astype(o_ref.dtype)


def _cast_bf16(w, layer):
    _, r, c = w.shape
    tr = _tile(r, 512, 2 * _SUBLANES)
    return pl.pallas_call(
        _cast_body,
        grid=(r // tr,),
        in_specs=[pl.BlockSpec((None, tr, c), lambda i: (layer, i, 0))],
        out_specs=pl.BlockSpec((None, tr, c), lambda i: (0, i, 0)),
        out_shape=jax.ShapeDtypeStruct((1, r, c), _BF16),
        compiler_params=_params("parallel"),
        name="cast_bf16",
    )(w)


def _ep_id(a):
    return a


def _ep_swiglu(g, u):
    return g * jax.nn.sigmoid(g) * u


def _ep_glu(a, b):
    return a * jax.nn.sigmoid(b)


def _ep_logsigmoid(a):
    return jnp.minimum(a, 0.0) - jnp.log(1.0 + jnp.exp(-jnp.abs(a)))


def _mod_body(c_ref, w_ref, b_ref, o_ref):
    c = c_ref[...]
    a = (c * jax.nn.sigmoid(c)).astype(_BF16)
    o_ref[...] = jnp.dot(a, w_ref[...].astype(_BF16), preferred_element_type=_F32) + b_ref[...]


def _modulation(c, mod_w, mod_b):
    depth, d, n = mod_w.shape
    rows = c.shape[0]
    tn = _tile(n, 512, _LANES)
    return pl.pallas_call(
        _mod_body,
        grid=(depth, n // tn),
        in_specs=[pl.BlockSpec((rows, d), lambda l, j: (0, 0)),
                  pl.BlockSpec((None, d, tn), lambda l, j: (l, 0, j)),
                  pl.BlockSpec((None, 1, tn), lambda l, j: (l, 0, j))],
        out_specs=pl.BlockSpec((None, rows, tn), lambda l, j: (l, 0, j)),
        out_shape=jax.ShapeDtypeStruct((depth, rows, n), _F32),
        compiler_params=_params("parallel", "arbitrary"),
        name="modulation",
    )(c, mod_w, mod_b.reshape(depth, 1, n))


def _rms(x, g):
    return x * lax.rsqrt(jnp.mean(x * x, axis=-1, keepdims=True) + RMS_EPS) * g


def _norm_mod_body(x_ref, g_ref, sc_ref, sh_ref, h_ref):
    y = _rms(x_ref[...], g_ref[...])
    h_ref[...] = (y * (1.0 + sc_ref[...]) + sh_ref[...]).astype(h_ref.dtype)


def _res_norm_body(x_ref, o_ref, gpost_ref, gate_ref, *rest, with_next):
    xn = x_ref[...] + gate_ref[...] * _rms(o_ref[...].astype(_F32), gpost_ref[...])
    if with_next:
        gpre_ref, sc_ref, sh_ref, xn_ref, h_ref = rest
        xn_ref[...] = xn
        h_ref[...] = (_rms(xn, gpre_ref[...]) * (1.0 + sc_ref[...]) + sh_ref[...]).astype(h_ref.dtype)
    else:
        (xn_ref,) = rest
        xn_ref[...] = xn


def _row_specs(rows, d, tr, seq):
    per = seq // tr
    row = pl.BlockSpec((tr, d), lambda i: (i, 0))
    vec = pl.BlockSpec((1, d), lambda i: (0, 0))
    bvec = pl.BlockSpec((None, 1, d), lambda i: (i // per, 0, 0))
    return row, vec, bvec


def _norm_mod(x, g, scale, shift, seq):
    rows, d = x.shape
    tr = _tile(seq, 256, _SUBLANES)
    row, vec, bvec = _row_specs(rows, d, tr, seq)
    return pl.pallas_call(
        _norm_mod_body,
        grid=(rows // tr,),
        in_specs=[row, vec, bvec, bvec],
        out_specs=row,
        out_shape=jax.ShapeDtypeStruct((rows, d), _BF16),
        compiler_params=_params("parallel"),
        name="norm_mod",
    )(x, g.reshape(1, d), scale, shift)


def _res_norm(x, o, gpost, gate, nxt, seq):
    rows, d = x.shape
    tr = _tile(seq, 256, _SUBLANES)
    row, vec, bvec = _row_specs(rows, d, tr, seq)
    xs = jax.ShapeDtypeStruct((rows, d), _F32)
    if nxt is None:
        return pl.pallas_call(
            functools.partial(_res_norm_body, with_next=False),
            grid=(rows // tr,),
            in_specs=[row, row, vec, bvec],
            out_specs=row,
            out_shape=xs,
            compiler_params=_params("parallel"),
            name="res_norm_last",
        )(x, o, gpost.reshape(1, d), gate), None
    g, scale, shift = nxt
    return pl.pallas_call(
        functools.partial(_res_norm_body, with_next=True),
        grid=(rows // tr,),
        in_specs=[row, row, vec, bvec, vec, bvec, bvec],
        out_specs=(row, row),
        out_shape=(xs, jax.ShapeDtypeStruct((rows, d), _BF16)),
        compiler_params=_params("parallel"),
        name="res_norm",
    )(x, o, gpost.reshape(1, d), gate, g.reshape(1, d), scale, shift)


def _cumsum_body(x_ref, o_ref, *, blk):
    seq = x_ref.shape[0]
    r = lax.broadcasted_iota(jnp.int32, (blk, blk), 0)
    c = lax.broadcasted_iota(jnp.int32, (blk, blk), 1)
    tri = (c <= r).astype(_F32)
    carry = jnp.zeros((1, x_ref.shape[1]), _F32)
    for i in range(seq // blk):
        y = jnp.dot(tri, x_ref[i * blk:(i + 1) * blk, :], precision=_HI,
                    preferred_element_type=_F32) + carry
        o_ref[i * blk:(i + 1) * blk, :] = y
        carry = y[blk - 1:blk, :]


def _cumsum_rows(x, seq):
    rows, n = x.shape
    blk = _tile(seq, 256, _SUBLANES)
    return pl.pallas_call(
        functools.partial(_cumsum_body, blk=blk),
        grid=(rows // seq,),
        in_specs=[pl.BlockSpec((seq, n), lambda b: (b, 0))],
        out_specs=pl.BlockSpec((seq, n), lambda b: (b, 0)),
        out_shape=jax.ShapeDtypeStruct((rows, n), _F32),
        compiler_params=_params("parallel"),
        name="logf_cumsum",
    )(x)


def _fox_prompt_body(q_ref, k_ref, v_ref, cq_ref, ck_ref, o_ref, kb_ref, vb_ref, *, tq, scale):
    h = pl.program_id(1)
    qi = pl.program_id(2)

    @pl.when(qi == 0)
    def _():
        kb_ref[...] = k_ref[...].astype(_BF16)
        vb_ref[...] = v_ref[...].astype(_BF16)

    dh = q_ref.shape[1]
    q = (q_ref[...] * scale).astype(_BF16)
    head_lane = lax.broadcasted_iota(jnp.int32, cq_ref.shape, 1) == h
    cq = jnp.sum(jnp.where(head_lane, cq_ref[...], 0.0), axis=1, keepdims=True)

    def step(kj, carry, masked):
        m, l, acc = carry
        k0 = pl.multiple_of(kj * tq, tq)
        k = kb_ref[pl.ds(k0, tq), :]
        v = vb_ref[pl.ds(k0, tq), :]
        s = lax.dot_general(q, k, _NT, preferred_element_type=_F32)
        s = s + (cq - ck_ref[pl.ds(h, 1), pl.ds(k0, tq)])
        if masked:
            row = lax.broadcasted_iota(jnp.int32, (tq, tq), 0)
            col = lax.broadcasted_iota(jnp.int32, (tq, tq), 1)
            s = jnp.where(col <= row, s, -jnp.inf)
        m_new = jnp.maximum(m, jnp.max(s, axis=1, keepdims=True))
        alpha = jnp.exp(m - m_new)
        p = jnp.exp(s - m_new)
        l = alpha * l + jnp.sum(p, axis=1, keepdims=True)
        acc = alpha * acc + jnp.dot(p.astype(_BF16), v, preferred_element_type=_F32)
        return m_new, l, acc

    init = (jnp.full((tq, 1), -jnp.inf, _F32), jnp.zeros((tq, 1), _F32), jnp.zeros((tq, dh), _F32))
    carry = lax.fori_loop(0, qi, lambda kj, c: step(kj, c, False), init)
    _, l, acc = step(qi, carry, True)
    o_ref[...] = (acc / l).astype(o_ref.dtype)


def _fox_prompt(q, k, v, cum_tok, cum_heads, bsz, seq, heads, dh):
    tq = _tile(seq, 512, _LANES)
    nq = seq // tq
    return pl.pallas_call(
        functools.partial(_fox_prompt_body, tq=tq, scale=dh ** -0.5),
        grid=(bsz, heads, nq),
        in_specs=[pl.BlockSpec((tq, dh), lambda b, h, i: (b * nq + i, h)),
                  pl.BlockSpec((seq, dh), lambda b, h, i: (b, h)),
                  pl.BlockSpec((seq, dh), lambda b, h, i: (b, h)),
                  pl.BlockSpec((tq, cum_tok.shape[1]), lambda b, h, i: (b * nq + i, 0)),
                  pl.BlockSpec((None, heads, seq), lambda b, h, i: (b, 0, 0))],
        out_specs=pl.BlockSpec((tq, dh), lambda b, h, i: (b * nq + i, h)),
        out_shape=jax.ShapeDtypeStruct((bsz * seq, heads * dh), _BF16),
        scratch_shapes=[pltpu.VMEM((seq, dh), _BF16), pltpu.VMEM((seq, dh), _BF16)],
        compiler_params=_params("parallel", "parallel", "arbitrary"),
        name="fox_prompt",
    )(q, k, v, cum_tok, cum_heads)


def _split_heads_body(x_ref, o_ref, *, heads, dh):
    rows = x_ref.shape[0]
    for h in range(heads):
        o_ref[pl.ds(h, rows, stride=heads), :] = x_ref[:, h * dh:(h + 1) * dh]


def _split_heads(x, heads, dh):
    m = x.shape[0]
    tr = _tile(m, 512, _SUBLANES)
    return pl.pallas_call(
        functools.partial(_split_heads_body, heads=heads, dh=dh),
        grid=(m // tr,),
        in_specs=[pl.BlockSpec((tr, heads * dh), lambda i: (i, 0))],
        out_specs=pl.BlockSpec((tr * heads, dh), lambda i: (i, 0)),
        out_shape=jax.ShapeDtypeStruct((m * heads, dh), x.dtype),
        compiler_params=_params("parallel"),
        name="split_heads",
    )(x)


def _fox_sample_body(pt_ref, q_ref, kn_ref, vn_ref, gn_ref, *rest, heads, nq, dh, pps):
    del pt_ref
    kc_refs, vc_refs, gc_refs = rest[:pps], rest[pps:2 * pps], rest[2 * pps:3 * pps]
    o_ref, m_ref, l_ref, acc_ref, carry_ref, lq_ref, mask_ref, s_ref = rest[3 * pps:]
    step = pl.program_id(1)
    hq = heads * nq
    grows = gc_refs[0].shape[0]
    lane1 = lax.broadcasted_iota(jnp.int32, (1, _LANES), 1)
    row_head = lax.broadcasted_iota(jnp.int32, (hq, _LANES), 0) // nq
    row_query = lax.broadcasted_iota(jnp.int32, (hq, _LANES), 0) % nq
    lane_head = lax.broadcasted_iota(jnp.int32, (hq, _LANES), 1) % heads
    lane_key = lax.broadcasted_iota(jnp.int32, (hq, _LANES), 1) // heads

    def update(s, vb):
        m_prev = m_ref[...]
        m_new = jnp.maximum(m_prev, jnp.max(s, axis=1, keepdims=True))
        alpha = jnp.exp(m_prev - m_new)
        p = jnp.exp(s - m_new)
        l_ref[...] = alpha * l_ref[...] + jnp.sum(p, axis=1, keepdims=True)
        acc_ref[...] = alpha * acc_ref[...] + jnp.dot(p.astype(_BF16), vb, preferred_element_type=_F32)
        m_ref[...] = m_new

    @pl.when(step == 0)
    def _():
        x = gn_ref[...]
        s_ = heads
        while s_ < _LANES:
            x = x + jnp.where(lane1 >= s_, pltpu.roll(x, s_, axis=1), 0.0)
            s_ *= 2
        own = lane_head == row_head
        lq = jnp.sum(jnp.where(own & (lane_key == row_query), x, 0.0), axis=1, keepdims=True)
        lq_ref[...] = lq
        carry_ref[...] = jnp.zeros_like(carry_ref)
        m_ref[...] = jnp.full_like(m_ref, -jnp.inf)
        l_ref[...] = jnp.zeros_like(l_ref)
        acc_ref[...] = jnp.zeros_like(acc_ref)
        neg = jnp.where(own, 0.0, -jnp.inf)
        for r in range(grows):
            mask_ref[:, r * _LANES:(r + 1) * _LANES] = neg
        s = lax.dot_general(q_ref[...], kn_ref[...].astype(_BF16), _NT, preferred_element_type=_F32)
        s = jnp.where(own & (lane_key <= row_query), s + (lq - x), -jnp.inf)
        update(s, vn_ref[...].astype(_BF16))

    @pl.when(step > 0)
    def _():
        row = lax.broadcasted_iota(jnp.int32, (grows, _LANES), 0)
        lane = lax.broadcasted_iota(jnp.int32, (grows, _LANES), 1)
        for p in reversed(range(pps)):
            g = gc_refs[p][...]
            within = g
            total = g
            s_ = heads
            while s_ < _LANES:
                within = within + jnp.where(lane < _LANES - s_, pltpu.roll(within, _LANES - s_, axis=1), 0.0)
                total = total + pltpu.roll(total, s_, axis=1)
                s_ *= 2
            below = total
            s_ = 1
            while s_ < grows:
                below = below + jnp.where(row < grows - s_, pltpu.roll(below, grows - s_, axis=0), 0.0)
                s_ *= 2
            carry = carry_ref[...]
            later = (within - g) + (below - total) + carry
            carry_ref[...] = carry + below[0:1, :]
            kb = kc_refs[p][...].astype(_BF16)
            vb = vc_refs[p][...].astype(_BF16)
            s_ref[...] = lax.dot_general(q_ref[...], kb, _NT, preferred_element_type=_F32)
            for r in range(grows):
                cols = slice(r * _LANES, (r + 1) * _LANES)
                s_ref[:, cols] = s_ref[:, cols] + later[r:r + 1, :]
            update(s_ref[...] + mask_ref[...] + lq_ref[...], vb)

    @pl.when(step == pl.num_programs(1) - 1)
    def _():
        out = acc_ref[...] / l_ref[...]
        for h in range(heads):
            o_ref[:, h * dh:(h + 1) * dh] = out[h * nq:(h + 1) * nq, :].astype(o_ref.dtype)


def _fox_sample(q, knew, vnew, gnew, cache_k, cache_v, cache_logf, page_table, layer, nq, heads, dh):
    bsz, n_pages = page_table.shape
    ne, n_pool, page = cache_k.shape[:3]
    width = heads * dh
    hq = heads * nq
    assert nq * heads == _LANES and _LANES % heads == 0
    kh = page * heads
    ck = cache_k.reshape(ne, n_pool, kh, dh)
    cv = cache_v.reshape(ne, n_pool, kh, dh)
    cg = cache_logf.reshape(ne, n_pool, kh // _LANES, _LANES)
    pps = _tile(n_pages, 4, 1)
    nblk = n_pages // pps

    def new_map(b, s, pt):
        return (b, 0, 0)

    def page_map(p):
        def index(b, s, pt):
            return (layer, pt[b * n_pages + (nblk - jnp.maximum(s, 1)) * pps + p], 0, 0)
        return index

    wide = [pl.BlockSpec((None, None, kh, dh), page_map(p)) for p in range(pps)]
    narrow = [pl.BlockSpec((None, None, kh // _LANES, _LANES), page_map(p)) for p in range(pps)]
    grid_spec = pltpu.PrefetchScalarGridSpec(
        num_scalar_prefetch=1,
        grid=(bsz, nblk + 1),
        in_specs=[pl.BlockSpec((None, hq, dh), new_map),
                  pl.BlockSpec((hq, dh), lambda b, s, pt: (b, 0)),
                  pl.BlockSpec((hq, dh), lambda b, s, pt: (b, 0)),
                  pl.BlockSpec((None, 1, _LANES), new_map)] + wide + wide + narrow,
        out_specs=pl.BlockSpec((None, nq, width), new_map),
        scratch_shapes=[pltpu.VMEM((hq, 1), _F32), pltpu.VMEM((hq, 1), _F32),
                        pltpu.VMEM((hq, dh), _F32), pltpu.VMEM((1, _LANES), _F32),
                        pltpu.VMEM((hq, 1), _F32),
                        pltpu.VMEM((hq, kh), _F32), pltpu.VMEM((hq, kh), _F32)],
    )
    return pl.pallas_call(
        functools.partial(_fox_sample_body, heads=heads, nq=nq, dh=dh, pps=pps),
        grid_spec=grid_spec,
        out_shape=jax.ShapeDtypeStruct((bsz, nq, width), _BF16),
        compiler_params=_params("parallel", "arbitrary"),
        name="fox_sample",
    )(page_table.reshape(-1), q, knew, vnew, gnew, *([ck] * pps), *([cv] * pps), *([cg] * pps))


def _gelu_tanh(y):
    return 0.5 * y * (1.0 + jnp.tanh(math.sqrt(2.0 / math.pi) * (y + 0.044715 * (y * y * y))))


def _tile_rows(x, n):
    return jnp.broadcast_to(x[None], (n,) + x.shape).reshape(n * x.shape[0], x.shape[1])


def _repeat_rows(x, n):
    return jnp.broadcast_to(x[:, None, :], (x.shape[0], n, x.shape[1])).reshape(x.shape[0] * n, x.shape[1])


def _s5_chunk_operators(bbr, bbi, cr, ci, crt, cit, prv, piv, prt, pit, glu, *, ch, t_len, ns):
    tc = t_len * ch
    per = _LANES // ch
    bbr_t, bbi_t = _tile_rows(bbr, t_len), _tile_rows(bbi, t_len)
    pr_r, pi_r = _repeat_rows(prv, ch), _repeat_rows(piv, ch)
    x_re = bbr_t * pr_r - bbi_t * pi_r
    x_im = bbr_t * pi_r + bbi_t * pr_r
    ws = jnp.concatenate([x_re, x_im], axis=1)
    k_rep = (lax.dot_general(x_re, _tile_rows(cr, per), _NT, precision=_HI, preferred_element_type=_F32)
             - lax.dot_general(x_im, _tile_rows(ci, per), _NT, precision=_HI, preferred_element_type=_F32))
    k_ext = jnp.concatenate([k_rep, jnp.zeros((tc, _LANES), _F32)], axis=0)
    lane_blk = lax.broadcasted_iota(jnp.int32, (tc, _LANES), 1) // ch
    cols = []
    for col in range(tc // _LANES):
        acc = jnp.zeros((tc, _LANES), _F32)
        for tt in range(per):
            r0 = ch * (t_len - 1 - (col * per + tt))
            acc = jnp.where(lane_blk == tt, k_ext[r0:r0 + tc, :], acc)
        cols.append(acc)
    wy = cols[0] if len(cols) == 1 else jnp.concatenate(cols, axis=1)
    e_mat = (lax.broadcasted_iota(jnp.int32, (t_len, tc), 0)
             == lax.broadcasted_iota(jnp.int32, (t_len, tc), 1) // ch).astype(_F32)
    f_mat = (lax.broadcasted_iota(jnp.int32, (ch, tc), 0)
             == lax.broadcasted_iota(jnp.int32, (ch, tc), 1) % ch).astype(_F32)
    hdot = functools.partial(jnp.dot, precision=_HI, preferred_element_type=_F32)
    pr1, pi1 = hdot(prt, e_mat), hdot(pit, e_mat)
    crx, cix = hdot(crt, f_mat), hdot(cit, f_mat)
    wh = jnp.concatenate([pr1 * crx - pi1 * cix, -(pi1 * crx + pr1 * cix)], axis=0)
    same_t = (lax.broadcasted_iota(jnp.int32, (tc, tc), 0) // ch
              == lax.broadcasted_iota(jnp.int32, (tc, tc), 1) // ch)
    wg = jnp.where(same_t, _tile_rows(hdot(glu, f_mat), t_len), 0.0)
    return wy, ws, wh, wg


def _s5_body(z_ref, bbr_ref, bbi_ref, cr_ref, ci_ref, crt_ref, cit_ref, prv_ref, piv_ref, prt_ref, pit_ref,
             glu_ref, dv_ref, gb_ref, a1_ref, a2_ref, h0_ref,
             o_ref, hf_ref, pin_ref, rin_ref, zacc_ref, ybuf_ref, *, gpb, ch, t_len, bsz, nc, ns):
    tc = t_len * ch
    rows = bsz * nc
    per = _LANES // ch
    ncol = tc // _LANES
    for t in range(t_len):
        pin_ref[t] = z_ref[pl.ds(t, rows, stride=t_len), :]
    slot = lax.broadcasted_iota(jnp.int32, (rows, _LANES), 1) // ch
    for col in range(ncol):
        for d in range(per):
            acc = jnp.zeros((rows, _LANES), _F32)
            for tt in range(per):
                acc = jnp.where(slot == (tt - d) % per, pin_ref[col * per + tt], acc)
            rin_ref[col * per + d] = acc if d == 0 else pltpu.roll(acc, d * ch, axis=1)
    zacc_ref[...] = jnp.zeros_like(zacc_ref)

    def one(g, _):
        wy, ws, wh, wg = _s5_chunk_operators(
            bbr_ref[g], bbi_ref[g], cr_ref[g], ci_ref[g], crt_ref[g], cit_ref[g], prv_ref[g], piv_ref[g],
            prt_ref[g], pit_ref[g], glu_ref[g], ch=ch, t_len=t_len, ns=ns)
        cols = []
        for col in range(ncol):
            acc = jnp.zeros((rows, _LANES), _F32)
            for tt in range(per):
                acc = jnp.where(slot == tt, rin_ref[col * per + (tt - g + per) % per], acc)
            cols.append(acc)
        u = cols[0] if ncol == 1 else jnp.concatenate(cols, axis=1)
        ub = u.astype(_BF16)
        yin = jnp.dot(ub, wy.astype(_BF16), preferred_element_type=_F32)
        a1 = a1_ref[g]
        a2 = a2_ref[g]
        if nc == 1:
            st = jnp.dot(u, ws, precision=_HI, preferred_element_type=_F32)
            hstart = h0_ref[g]
            hf_ref[g] = a1[0:1] * hstart + a2[0:1] * pltpu.roll(hstart, ns, axis=1) + st
        else:
            x = jnp.dot(ub, ws.astype(_BF16), preferred_element_type=_F32)
            chunk = lax.broadcasted_iota(jnp.int32, (rows, 2 * ns), 0) % nc
            for k in range(nc.bit_length() - 1):
                sh = 1 << k
                xs = pltpu.roll(x, sh, axis=0)
                upd = a1[k:k + 1] * xs + a2[k:k + 1] * pltpu.roll(xs, ns, axis=1)
                x = x + jnp.where(chunk >= sh, upd, 0.0)
            hstart = jnp.where(chunk >= 1, pltpu.roll(x, 1, axis=0), 0.0)
            for b in range(bsz):
                hf_ref[g, b:b + 1, :] = x[(b + 1) * nc - 1:(b + 1) * nc, :]
        y = yin + jnp.dot(hstart.astype(_BF16), wh.astype(_BF16), preferred_element_type=_F32)
        y = _gelu_tanh(y + dv_ref[g] * u)
        gate = jax.nn.sigmoid(jnp.dot(y.astype(_BF16), wg.astype(_BF16),
                                      preferred_element_type=_F32) + gb_ref[g])
        out = y * gate
        for col in range(ncol):
            piece = out[:, col * _LANES:(col + 1) * _LANES]
            for d in range(per):
                zacc_ref[col * per + d] = jnp.where(slot == (g - d + per) % per, piece, zacc_ref[col * per + d])
        return 0

    lax.fori_loop(0, gpb, one, 0)
    for col in range(ncol):
        turned = [zacc_ref[col * per + d] if d == 0 else pltpu.roll(zacc_ref[col * per + d], d * ch, axis=1)
                  for d in range(per)]
        for tt in range(per):
            acc = jnp.zeros((rows, _LANES), _F32)
            for d in range(per):
                acc = jnp.where(slot == (tt + d) % per, turned[d], acc)
            ybuf_ref[pl.ds(col * per + tt, rows, stride=t_len), :] = acc
    o_ref[...] = ybuf_ref[...].astype(o_ref.dtype)


def _s5_operators(lam_re, lam_im, log_step, b_re, b_im, c_re, c_im, d, glu_w, glu_b, t_len, n_scan):
    groups, ns = lam_re.shape
    dt = jnp.exp(log_step)[:, None]

    def power(tau):
        tau = tau.astype(_F32)[None, :, None]
        mag = jnp.exp(lam_re[:, None, :] * dt[:, None, :] * tau)
        ang = lam_im[:, None, :] * dt[:, None, :] * tau
        return mag * jnp.cos(ang), mag * jnp.sin(ang)

    ab_re, ab_im = (p[:, 0] for p in power(jnp.ones((1,))))
    nr, ni = ab_re - 1.0, ab_im
    den = lam_re * lam_re + lam_im * lam_im
    f_re = (nr * lam_re + ni * lam_im) / den
    f_im = (ni * lam_re - nr * lam_im) / den
    bb_re = f_re[..., None] * b_re - f_im[..., None] * b_im
    bb_im = f_re[..., None] * b_im + f_im[..., None] * b_re
    pr, pi = power(jnp.arange(t_len + 1))
    rev = t_len - 1 - jnp.arange(t_len)
    dv = jnp.tile(d, (1, t_len))[:, None, :]
    gb = jnp.tile(glu_b, (1, t_len))[:, None, :]
    sr, si = power(t_len * (2 ** jnp.arange(_SUBLANES)))
    keep = (jnp.arange(_SUBLANES) < max(n_scan, 1))[None, :, None]
    sr, si = jnp.where(keep, sr, 0.0), jnp.where(keep, si, 0.0)
    a1 = jnp.concatenate([sr, sr], axis=-1)
    a2 = jnp.concatenate([-si, si], axis=-1)
    return (bb_re.transpose(0, 2, 1), bb_im.transpose(0, 2, 1), c_re, c_im,
            c_re.transpose(0, 2, 1), c_im.transpose(0, 2, 1), pr[:, rev], pi[:, rev],
            pr[:, 1:].transpose(0, 2, 1), pi[:, 1:].transpose(0, 2, 1), glu_w, dv, gb, a1, a2)


def _s5(u, h0, ops, bsz, seq, t_len):
    groups, ch, ns = ops[0].shape
    ns2 = 2 * ns
    tc = t_len * ch
    nc = seq // t_len
    rows = bsz * nc
    total = bsz * seq
    assert tc % _LANES == 0 and _LANES % ch == 0
    gpb = _LANES // ch
    if h0 is None:
        h0 = jnp.zeros((groups, bsz, ns2), _F32)

    def gmap(i):
        return (i, 0, 0)

    def per_group(a):
        return pl.BlockSpec((gpb,) + a.shape[1:], gmap)

    lane_block = pl.BlockSpec((total, _LANES), lambda i: (0, i))
    by_time = pltpu.VMEM((t_len, rows, _LANES), _F32)
    return pl.pallas_call(
        functools.partial(_s5_body, gpb=gpb, ch=ch, t_len=t_len, bsz=bsz, nc=nc, ns=ns),
        grid=(groups // gpb,),
        in_specs=[lane_block] + [per_group(a) for a in ops] + [per_group(h0)],
        out_specs=(lane_block, pl.BlockSpec((gpb, bsz, ns2), gmap)),
        out_shape=(jax.ShapeDtypeStruct((total, groups * ch), _BF16),
                   jax.ShapeDtypeStruct((groups, bsz, ns2), _F32)),
        scratch_shapes=[by_time, by_time, by_time, pltpu.VMEM((total, _LANES), _F32)],
        compiler_params=_params("arbitrary"),
        name="s5_chunks",
    )(u, *ops, h0)


def _conv_body(u_ref, halo_ref, st_ref, w_ref, wb_ref, lg_ref, lb_ref, o_ref, win_ref, sh_ref, y_ref,
               *, nblk, taps, rc, cc):
    i = pl.program_id(0)
    rows, d = u_ref.shape
    hb = halo_ref.shape[0]
    span = hb + rows
    first = (i % nblk) == 0

    @pl.when(first)
    def _():
        win_ref[0:hb, :] = st_ref[...]

    @pl.when(jnp.logical_not(first))
    def _():
        win_ref[0:hb, :] = halo_ref[...]

    win_ref[hb:span, :] = u_ref[...]
    win_ref[span:span + _SUBLANES, :] = jnp.zeros((_SUBLANES, d), _F32)
    off = hb - (taps - 1)

    def col_loop(ci, _):
        c0 = pl.multiple_of(ci * cc, cc)
        for r in range(_SUBLANES):
            sh_ref[r] = win_ref[pl.ds(r, span), pl.ds(c0, cc)]
        bias = wb_ref[:, pl.ds(c0, cc)]
        for r0 in range(0, rows, rc):
            acc = jnp.zeros((rc, cc), _F32) + bias
            for j in range(taps):
                a, r = divmod(off + j, _SUBLANES)
                acc = acc + sh_ref[r, pl.ds(r0 + _SUBLANES * a, rc), :] * w_ref[pl.ds(j, 1), pl.ds(c0, cc)]
            y_ref[r0:r0 + rc, pl.ds(c0, cc)] = acc
        return 0

    lax.fori_loop(0, d // cc, col_loop, 0)
    y = y_ref[...]
    mu = jnp.mean(y, axis=-1, keepdims=True)
    yc = y - mu
    var = jnp.mean(yc * yc, axis=-1, keepdims=True)
    z = yc * lax.rsqrt(var + LN_EPS) * lg_ref[...] + lb_ref[...]
    o_ref[...] = (z * jax.nn.sigmoid(z)).astype(o_ref.dtype)


def _conv_ln_silu(u, state, dw, dw_b, ln_g, ln_b, seq):
    rows_total, d = u.shape
    taps = dw.shape[0]
    hb = -(-(taps - 1) // _SUBLANES) * _SUBLANES
    st = jnp.pad(state.astype(_F32), ((0, 0), (hb - (taps - 1), 0), (0, 0)))
    rows = _tile(seq, 256, hb) if seq % hb == 0 else seq
    nblk = seq // rows
    rc = _tile(rows, 64, _SUBLANES)
    cc = _tile(d, 256, _LANES)
    per = rows // hb if nblk > 1 else 1
    halo_src = u if nblk > 1 else st[0]

    def halo_map(i):
        return (jnp.maximum(i * per - 1, 0) if nblk > 1 else 0, 0)

    vec = pl.BlockSpec((1, d), lambda i: (0, 0))
    return pl.pallas_call(
        functools.partial(_conv_body, nblk=nblk, taps=taps, rc=rc, cc=cc),
        grid=(rows_total // rows,),
        in_specs=[pl.BlockSpec((rows, d), lambda i: (i, 0)),
                  pl.BlockSpec((hb, d), halo_map),
                  pl.BlockSpec((None, hb, d), lambda i: (i // nblk, 0, 0)),
                  pl.BlockSpec((taps, d), lambda i: (0, 0)),
                  vec, vec, vec],
        out_specs=pl.BlockSpec((rows, d), lambda i: (i, 0)),
        out_shape=jax.ShapeDtypeStruct((rows_total, d), _BF16),
        scratch_shapes=[pltpu.VMEM((hb + rows + _SUBLANES, d), _F32),
                        pltpu.VMEM((_SUBLANES, hb + rows, cc), _F32),
                        pltpu.VMEM((rows, d), _F32)],
        compiler_params=_params("parallel"),
        name="conv_ln_silu",
    )(u, halo_src, st, dw, dw_b.reshape(1, d), ln_g.reshape(1, d), ln_b.reshape(1, d))


def kernel(x_prompt, x_sample, cache_k, cache_v, cache_logf, state_s5_re, state_s5_im, state_conv, page_table, c_prompt, c_sample, mod_w, mod_b, norm_mix_pre, norm_mix_post, norm_ffn_pre, norm_ffn_post, ffn_w_gate, ffn_w_up, ffn_w_down, hyb_w_in, hyb_b_f, hyb_w_out, s5_lambda_re, s5_lambda_im, s5_log_step, s5_b_re, s5_b_im, s5_c_re, s5_c_im, s5_d, s5_glu_w, s5_glu_b, conv_w_in, conv_b_in, conv_dw, conv_dw_b, conv_ln_g, conv_ln_b, conv_w_out):
    bp, lp, d = x_prompt.shape
    bs, ls, _ = x_sample.shape
    depth = mod_w.shape[0]
    heads, dh = cache_k.shape[3], cache_k.shape[4]
    fw = heads * dh
    ns = s5_lambda_re.shape[2]
    s5w = s5_b_re.shape[1] * s5_b_re.shape[-1]
    taps = conv_dw.shape[1]
    hidden = ffn_w_gate.shape[2]
    seqs = ((bp, lp), (bs, ls))

    n_c = bp + bs
    c_rows = -(-n_c // (2 * _SUBLANES)) * (2 * _SUBLANES)
    c_all = jnp.pad(jnp.concatenate([c_prompt, c_sample], axis=0), ((0, c_rows - n_c), (0, 0)))
    mod = _modulation(c_all, mod_w, mod_b).reshape(depth, c_rows, 6, d)

    def mod_terms(i, grp):
        lo, hi = (0, bp) if grp == 0 else (bp, bp + bs)
        return [mod[i, lo:hi, t][:, None, :] for t in range(6)]

    xs = [x_prompt.reshape(bp * lp, d), x_sample.reshape(bs * ls, d)]
    terms = [mod_terms(0, g) for g in range(2)]
    hs = [_norm_mod(xs[g], norm_mix_pre[0], terms[g][1], terms[g][0], seqs[g][1]) for g in range(2)]

    outs = {k: ([], []) for k in ("k", "v", "f", "s5r", "s5i", "conv")}
    conv_bias = conv_b_in.reshape(conv_b_in.shape[0], 1, conv_b_in.shape[1])

    for i in range(depth):
        j = i // 2
        if i % 2 == 0:
            w_u = hyb_w_in[j][:, 3 * fw + heads:][None]
            w_f = jnp.pad(hyb_w_in[j][:, 3 * fw:3 * fw + heads], ((0, 0), (0, _LANES - heads)))[None]
            b_f = jnp.pad(hyb_b_f[j], (0, _LANES - heads)).reshape(1, 1, _LANES)
            s5_raw = (s5_lambda_re[j], s5_lambda_im[j], s5_log_step[j], s5_b_re[j], s5_b_im[j],
                      s5_c_re[j], s5_c_im[j], s5_d[j], s5_glu_w[j], s5_glu_b[j])
            mixed = []
            for g, (bsz, seq) in enumerate(seqs):
                q, k, v = _mm([hs[g]], [(hyb_w_in, j, 0)], [], 3 * fw, _ep_id, _F32, 1024, 512, "in_proj_qkv",
                              parts=3)
                uz = _mm([hs[g]], [(w_u, 0, 0)], [], s5w, _ep_id, _F32, 1024, 512, "in_proj_s5")
                logf = _mm([hs[g]], [(w_f, 0, 0)], [(b_f, 0, 0)], _LANES, _ep_logsigmoid, _F32, 1024, _LANES,
                           "forget_gate")
                k_out, v_out = _split_heads(k, heads, dh), _split_heads(v, heads, dh)
                outs["k"][g].append(k_out.reshape(bsz, seq, heads, dh))
                outs["v"][g].append(v_out.reshape(bsz, seq, heads, dh))
                outs["f"][g].append(logf[:, :heads].reshape(bsz, seq, heads))
                if g == 0:
                    cum = _cumsum_rows(logf, seq)
                    cum_heads = cum[:, :heads].reshape(bsz, seq, heads).transpose(0, 2, 1)
                    att = _fox_prompt(q, k, v, cum, cum_heads, bsz, seq, heads, dh)
                    t_len = _S5_CHUNK if seq % _S5_CHUNK == 0 else seq
                    h0 = None
                else:
                    qs = (q.reshape(bsz, seq, heads, dh) * (dh ** -0.5)).transpose(0, 2, 1, 3)
                    qs = qs.reshape(bsz, heads * seq, dh).astype(_BF16)
                    gnew = logf[:, :heads].reshape(bsz, 1, seq * heads)
                    att = _fox_sample(qs, k_out, v_out, gnew, cache_k, cache_v, cache_logf, page_table,
                                      j, seq, heads, dh).reshape(bsz * seq, fw)
                    t_len = seq
                    h0 = jnp.concatenate([state_s5_re[j], state_s5_im[j]], axis=-1).transpose(1, 0, 2)
                n_scan = (seq // t_len).bit_length() - 1
                ops = _s5_operators(*s5_raw, t_len, n_scan)
                ssm, hf = _s5(uz, h0, ops, bsz, seq, t_len)
                hf = hf.transpose(1, 0, 2)
                outs["s5r"][g].append(hf[..., :ns])
                outs["s5i"][g].append(hf[..., ns:])
                mixed.append(_mm([att, ssm], [(hyb_w_out, j, 0)], [], d, _ep_id, _F32, 1024, 512, "out_proj"))
        else:
            mixed = []
            for g, (bsz, seq) in enumerate(seqs):
                u = _mm([hs[g]], [(conv_w_in, j, 0), (conv_w_in, j, d)], [(conv_bias, j, 0), (conv_bias, j, d)],
                        d, _ep_glu, _F32, 1024, 256, "conv_in_glu")
                state = jnp.zeros((bsz, taps - 1, d), _F32) if g == 0 else state_conv[j]
                u3 = u.reshape(bsz, seq, d)
                if seq >= taps - 1:
                    outs["conv"][g].append(u3[:, seq - (taps - 1):])
                else:
                    outs["conv"][g].append(jnp.concatenate([state[:, seq:], u3], axis=1))
                y = _conv_ln_silu(u, state, conv_dw[j], conv_dw_b[j], conv_ln_g[j], conv_ln_b[j], seq)
                mixed.append(_mm([y], [(conv_w_out, j, 0)], [], d, _ep_id, _F32, 1024, 512, "conv_out"))

        nxt_terms = [mod_terms(i + 1, g) for g in range(2)] if i + 1 < depth else None
        w_down = _cast_bf16(ffn_w_down, i)
        for g, (bsz, seq) in enumerate(seqs):
            t = terms[g]
            x1, h2 = _res_norm(xs[g], mixed[g], norm_mix_post[i], t[2], (norm_ffn_pre[i], t[4], t[3]), seq)
            a = _mm([h2], [(ffn_w_gate, i, 0), (ffn_w_up, i, 0)], [], hidden, _ep_swiglu, _BF16, 1024, 256,
                    "ffn_gate_up")
            f = _mm([a], [(w_down, 0, 0)], [], d, _ep_id, _F32, 512, 256, "ffn_down")
            nxt = None if nxt_terms is None else (norm_mix_pre[i + 1], nxt_terms[g][1], nxt_terms[g][0])
            xs[g], hs[g] = _res_norm(x1, f, norm_ffn_post[i], t[5], nxt, seq)
        terms = nxt_terms

    def stack(key, g):
        parts = outs[key][g]
        return parts[0][None] if len(parts) == 1 else jnp.stack(parts)

    return (xs[0].reshape(bp, lp, d), xs[1].reshape(bs, ls, d),
            stack("k", 0), stack("v", 0), stack("f", 0),
            stack("k", 1), stack("v", 1), stack("f", 1),
            stack("s5r", 0), stack("s5i", 0), stack("s5r", 1), stack("s5i", 1),
            stack("conv", 0), stack("conv", 1))
```

```python
import functools
import math

import jax
import jax.numpy as jnp
from jax import lax
from jax.experimental import pallas as pl
from jax.experimental.pallas import tpu as pltpu

_BF16 = jnp.bfloat16
_F32 = jnp.float32
_HI = lax.Precision.HIGHEST

RMS_EPS = 1e-6
LN_EPS = 1e-5

_V7X_VMEM_BYTES = 64 * 1024 * 1024
_LANES = 128
_SUBLANES = 8
_VMEM_LIMIT = _V7X_VMEM_BYTES - 8 * 1024 * 1024
_S5_CHUNK = 16
_NT = (((1,), (1,)), ((), ()))


def _params(*sem):
    return pltpu.CompilerParams(dimension_semantics=sem, vmem_limit_bytes=_VMEM_LIMIT)


def _tile(dim, pref, align):
    if dim <= pref:
        return dim
    t = (pref // align) * align
    while t >= align:
        if dim % t == 0:
            return t
        t -= align
    return dim


def _mm_body(*refs, n_x, n_w, n_b, epilogue):
    x_refs = refs[:n_x]
    w_refs = refs[n_x:n_x + n_x * n_w]
    b_refs = refs[n_x + n_x * n_w:n_x + n_x * n_w + n_b]
    o_ref = refs[-1]
    xs = [x[...] for x in x_refs]
    accs = []
    for i in range(n_w):
        parts = [jnp.dot(x, w[...].astype(_BF16), preferred_element_type=_F32)
                 for x, w in zip(xs, w_refs[i * n_x:(i + 1) * n_x])]
        accs.append(functools.reduce(lambda a, b: a + b, parts))
    if n_b:
        accs = [a + b[...] for a, b in zip(accs, b_refs)]
    o_ref[...] = epilogue(*accs).astype(o_ref.dtype)


def _mm(xs, ws, bs, n, epilogue, out_dtype, tm, tn, name, parts=1):
    m, k = xs[0].shape
    assert all(x.shape == (m, k) for x in xs)
    tm = _tile(m, tm, 2 * _SUBLANES)
    tn = _tile(n // parts, tn, _LANES)
    per_part = n // parts // tn

    def spec(rows, row_blk, layer, col0):
        assert col0 % tn == 0
        return pl.BlockSpec((None, rows, tn), lambda i, j: (layer, row_blk, col0 // tn + j))

    in_specs = ([pl.BlockSpec((tm, k), lambda i, j: (i, 0))] * len(xs)
                + [spec(k, r, layer, col0) for _, layer, col0 in ws for r in range(len(xs))]
                + [spec(1, 0, layer, col0) for _, layer, col0 in bs])
    out = pl.pallas_call(
        functools.partial(_mm_body, n_x=len(xs), n_w=len(ws), n_b=len(bs), epilogue=epilogue),
        grid=(m // tm, n // tn),
        in_specs=in_specs,
        out_specs=pl.BlockSpec((None, tm, tn), lambda i, j: (j // per_part, i, j % per_part)),
        out_shape=jax.ShapeDtypeStruct((parts, m, n // parts), out_dtype),
        compiler_params=_params("parallel", "arbitrary"),
        name=name,
    )(*xs, *[w for w, _, _ in ws for _ in xs], *[b for b, _, _ in bs])
    return out if parts > 1 else out[0]


def _cast_body(x_ref, o_ref):
    o_ref[...] = x_ref[...].astype(o_ref.dtype)


def _cast_bf16(w, layer):
    _, r, c = w.shape
    tr = _tile(r, 512, 2 * _SUBLANES)
    return pl.pallas_call(
        _cast_body,
        grid=(r // tr,),
        in_specs=[pl.BlockSpec((None, tr, c), lambda i: (layer, i, 0))],
        out_specs=pl.BlockSpec((None, tr, c), lambda i: (0, i, 0)),
        out_shape=jax.ShapeDtypeStruct((1, r, c), _BF16),
        compiler_params=_params("parallel"),
        name="cast_bf16",
    )(w)


def _ep_id(a):
    return a


def _ep_swiglu(g, u):
    return g * jax.nn.sigmoid(g) * u


def _ep_glu(a, b):
    return a * jax.nn.sigmoid(b)


def _ep_logsigmoid(a):
    return jnp.minimum(a, 0.0) - jnp.log(1.0 + jnp.exp(-jnp.abs(a)))


def _mod_body(c_ref, w_ref, b_ref, o_ref):
    c = c_ref[...]
    a = (c * jax.nn.sigmoid(c)).astype(_BF16)
    o_ref[...] = jnp.dot(a, w_ref[...].astype(_BF16), preferred_element_type=_F32) + b_ref[...]


def _modulation(c, mod_w, mod_b):
    depth, d, n = mod_w.shape
    rows = c.shape[0]
    tn = _tile(n, 512, _LANES)
    return pl.pallas_call(
        _mod_body,
        grid=(depth, n // tn),
        in_specs=[pl.BlockSpec((rows, d), lambda l, j: (0, 0)),
                  pl.BlockSpec((None, d, tn), lambda l, j: (l, 0, j)),
                  pl.BlockSpec((None, 1, tn), lambda l, j: (l, 0, j))],
        out_specs=pl.BlockSpec((None, rows, tn), lambda l, j: (l, 0, j)),
        out_shape=jax.ShapeDtypeStruct((depth, rows, n), _F32),
        compiler_params=_params("parallel", "arbitrary"),
        name="modulation",
    )(c, mod_w, mod_b.reshape(depth, 1, n))


def _rms(x, g):
    return x * lax.rsqrt(jnp.mean(x * x, axis=-1, keepdims=True) + RMS_EPS) * g


def _norm_mod_body(x_ref, g_ref, sc_ref, sh_ref, h_ref):
    y = _rms(x_ref[...], g_ref[...])
    h_ref[...] = (y * (1.0 + sc_ref[...]) + sh_ref[...]).astype(h_ref.dtype)


def _res_norm_body(x_ref, o_ref, gpost_ref, gate_ref, *rest, with_next):
    xn = x_ref[...] + gate_ref[...] * _rms(o_ref[...].astype(_F32), gpost_ref[...])
    if with_next:
        gpre_ref, sc_ref, sh_ref, xn_ref, h_ref = rest
        xn_ref[...] = xn
        h_ref[...] = (_rms(xn, gpre_ref[...]) * (1.0 + sc_ref[...]) + sh_ref[...]).astype(h_ref.dtype)
    else:
        (xn_ref,) = rest
        xn_ref[...] = xn


def _row_specs(rows, d, tr, seq):
    per = seq // tr
    row = pl.BlockSpec((tr, d), lambda i: (i, 0))
    vec = pl.BlockSpec((1, d), lambda i: (0, 0))
    bvec = pl.BlockSpec((None, 1, d), lambda i: (i // per, 0, 0))
    return row, vec, bvec


def _norm_mod(x, g, scale, shift, seq):
    rows, d = x.shape
    tr = _tile(seq, 256, _SUBLANES)
    row, vec, bvec = _row_specs(rows, d, tr, seq)
    return pl.pallas_call(
        _norm_mod_body,
        grid=(rows // tr,),
        in_specs=[row, vec, bvec, bvec],
        out_specs=row,
        out_shape=jax.ShapeDtypeStruct((rows, d), _BF16),
        compiler_params=_params("parallel"),
        name="norm_mod",
    )(x, g.reshape(1, d), scale, shift)


def _res_norm(x, o, gpost, gate, nxt, seq):
    rows, d = x.shape
    tr = _tile(seq, 256, _SUBLANES)
    row, vec, bvec = _row_specs(rows, d, tr, seq)
    xs = jax.ShapeDtypeStruct((rows, d), _F32)
    if nxt is None:
        return pl.pallas_call(
            functools.partial(_res_norm_body, with_next=False),
            grid=(rows // tr,),
            in_specs=[row, row, vec, bvec],
            out_specs=row,
            out_shape=xs,
            compiler_params=_params("parallel"),
            name="res_norm_last",
        )(x, o, gpost.reshape(1, d), gate), None
    g, scale, shift = nxt
    return pl.pallas_call(
        functools.partial(_res_norm_body, with_next=True),
        grid=(rows // tr,),
        in_specs=[row, row, vec, bvec, vec, bvec, bvec],
        out_specs=(row, row),
        out_shape=(xs, jax.ShapeDtypeStruct((rows, d), _BF16)),
        compiler_params=_params("parallel"),
        name="res_norm",
    )(x, o, gpost.reshape(1, d), gate, g.reshape(1, d), scale, shift)


def _cumsum_body(x_ref, o_ref, *, blk):
    seq = x_ref.shape[0]
    r = lax.broadcasted_iota(jnp.int32, (blk, blk), 0)
    c = lax.broadcasted_iota(jnp.int32, (blk, blk), 1)
    tri = (c <= r).astype(_F32)
    carry = jnp.zeros((1, x_ref.shape[1]), _F32)
    for i in range(seq // blk):
        y = jnp.dot(tri, x_ref[i * blk:(i + 1) * blk, :], precision=_HI,
                    preferred_element_type=_F32) + carry
        o_ref[i * blk:(i + 1) * blk, :] = y
        carry = y[blk - 1:blk, :]


def _cumsum_rows(x, seq):
    rows, n = x.shape
    blk = _tile(seq, 256, _SUBLANES)
    return pl.pallas_call(
        functools.partial(_cumsum_body, blk=blk),
        grid=(rows // seq,),
        in_specs=[pl.BlockSpec((seq, n), lambda b: (b, 0))],
        out_specs=pl.BlockSpec((seq, n), lambda b: (b, 0)),
        out_shape=jax.ShapeDtypeStruct((rows, n), _F32),
        compiler_params=_params("parallel"),
        name="logf_cumsum",
    )(x)


def _fox_prompt_body(q_ref, k_ref, v_ref, cq_ref, ck_ref, o_ref, kb_ref, vb_ref, *, tq, scale):
    h = pl.program_id(1)
    qi = pl.program_id(2)

    @pl.when(qi == 0)
    def _():
        kb_ref[...] = k_ref[...].astype(_BF16)
        vb_ref[...] = v_ref[...].astype(_BF16)

    dh = q_ref.shape[1]
    q = (q_ref[...] * scale).astype(_BF16)
    head_lane = lax.broadcasted_iota(jnp.int32, cq_ref.shape, 1) == h
    cq = jnp.sum(jnp.where(head_lane, cq_ref[...], 0.0), axis=1, keepdims=True)

    def step(kj, carry, masked):
        m, l, acc = carry
        k0 = pl.multiple_of(kj * tq, tq)
        k = kb_ref[pl.ds(k0, tq), :]
        v = vb_ref[pl.ds(k0, tq), :]
        s = lax.dot_general(q, k, _NT, preferred_element_type=_F32)
        s = s + (cq - ck_ref[pl.ds(h, 1), pl.ds(k0, tq)])
        if masked:
            row = lax.broadcasted_iota(jnp.int32, (tq, tq), 0)
            col = lax.broadcasted_iota(jnp.int32, (tq, tq), 1)
            s = jnp.where(col <= row, s, -jnp.inf)
        m_new = jnp.maximum(m, jnp.max(s, axis=1, keepdims=True))
        alpha = jnp.exp(m - m_new)
        p = jnp.exp(s - m_new)
        l = alpha * l + jnp.sum(p, axis=1, keepdims=True)
        acc = alpha * acc + jnp.dot(p.astype(_BF16), v, preferred_element_type=_F32)
        return m_new, l, acc

    init = (jnp.full((tq, 1), -jnp.inf, _F32), jnp.zeros((tq, 1), _F32), jnp.zeros((tq, dh), _F32))
    carry = lax.fori_loop(0, qi, lambda kj, c: step(kj, c, False), init)
    _, l, acc = step(qi, carry, True)
    o_ref[...] = (acc / l).astype(o_ref.dtype)


def _fox_prompt(q, k, v, cum_tok, cum_heads, bsz, seq, heads, dh):
    tq = _tile(seq, 512, _LANES)
    nq = seq // tq
    return pl.pallas_call(
        functools.partial(_fox_prompt_body, tq=tq, scale=dh ** -0.5),
        grid=(bsz, heads, nq),
        in_specs=[pl.BlockSpec((tq, dh), lambda b, h, i: (b * nq + i, h)),
                  pl.BlockSpec((seq, dh), lambda b, h, i: (b, h)),
                  pl.BlockSpec((seq, dh), lambda b, h, i: (b, h)),
                  pl.BlockSpec((tq, cum_tok.shape[1]), lambda b, h, i: (b * nq + i, 0)),
                  pl.BlockSpec((None, heads, seq), lambda b, h, i: (b, 0, 0))],
        out_specs=pl.BlockSpec((tq, dh), lambda b, h, i: (b * nq + i, h)),
        out_shape=jax.ShapeDtypeStruct((bsz * seq, heads * dh), _BF16),
        scratch_shapes=[pltpu.VMEM((seq, dh), _BF16), pltpu.VMEM((seq, dh), _BF16)],
        compiler_params=_params("parallel", "parallel", "arbitrary"),
        name="fox_prompt",
    )(q, k, v, cum_tok, cum_heads)


def _split_heads_body(x_ref, o_ref, *, heads, dh):
    rows = x_ref.shape[0]
    for h in range(heads):
        o_ref[pl.ds(h, rows, stride=heads), :] = x_ref[:, h * dh:(h + 1) * dh]


def _split_heads(x, heads, dh):
    m = x.shape[0]
    tr = _tile(m, 512, _SUBLANES)
    return pl.pallas_call(
        functools.partial(_split_heads_body, heads=heads, dh=dh),
        grid=(m // tr,),
        in_specs=[pl.BlockSpec((tr, heads * dh), lambda i: (i, 0))],
        out_specs=pl.BlockSpec((tr * heads, dh), lambda i: (i, 0)),
        out_shape=jax.ShapeDtypeStruct((m * heads, dh), x.dtype),
        compiler_params=_params("parallel"),
        name="split_heads",
    )(x)


def _fox_sample_body(pt_ref, q_ref, kn_ref, vn_ref, gn_ref, *rest, heads, nq, dh, pps):
    del pt_ref
    kc_refs, vc_refs, gc_refs = rest[:pps], rest[pps:2 * pps], rest[2 * pps:3 * pps]
    o_ref, m_ref, l_ref, acc_ref, carry_ref, lq_ref, mask_ref, s_ref = rest[3 * pps:]
    step = pl.program_id(1)
    hq = heads * nq
    grows = gc_refs[0].shape[0]
    lane1 = lax.broadcasted_iota(jnp.int32, (1, _LANES), 1)
    row_head = lax.broadcasted_iota(jnp.int32, (hq, _LANES), 0) // nq
    row_query = lax.broadcasted_iota(jnp.int32, (hq, _LANES), 0) % nq
    lane_head = lax.broadcasted_iota(jnp.int32, (hq, _LANES), 1) % heads
    lane_key = lax.broadcasted_iota(jnp.int32, (hq, _LANES), 1) // heads

    def update(s, vb):
        m_prev = m_ref[...]
        m_new = jnp.maximum(m_prev, jnp.max(s, axis=1, keepdims=True))
        alpha = jnp.exp(m_prev - m_new)
        p = jnp.exp(s - m_new)
        l_ref[...] = alpha * l_ref[...] + jnp.sum(p, axis=1, keepdims=True)
        acc_ref[...] = alpha * acc_ref[...] + jnp.dot(p.astype(_BF16), vb, preferred_element_type=_F32)
        m_ref[...] = m_new

    @pl.when(step == 0)
    def _():
        x = gn_ref[...]
        s_ = heads
        while s_ < _LANES:
            x = x + jnp.where(lane1 >= s_, pltpu.roll(x, s_, axis=1), 0.0)
            s_ *= 2
        own = lane_head == row_head
        lq = jnp.sum(jnp.where(own & (lane_key == row_query), x, 0.0), axis=1, keepdims=True)
        lq_ref[...] = lq
        carry_ref[...] = jnp.zeros_like(carry_ref)
        m_ref[...] = jnp.full_like(m_ref, -jnp.inf)
        l_ref[...] = jnp.zeros_like(l_ref)
        acc_ref[...] = jnp.zeros_like(acc_ref)
        base = jnp.where(own, lq, -jnp.inf)
        for r in range(grows):
            mask_ref[:, r * _LANES:(r + 1) * _LANES] = base
        s = lax.dot_general(q_ref[...], kn_ref[...].astype(_BF16), _NT, preferred_element_type=_F32)
        s = jnp.where(own & (lane_key <= row_query), s + (lq - x), -jnp.inf)
        update(s, vn_ref[...].astype(_BF16))

    @pl.when(step > 0)
    def _():
        row = lax.broadcasted_iota(jnp.int32, (grows, _LANES), 0)
        lane = lax.broadcasted_iota(jnp.int32, (grows, _LANES), 1)
        for p in reversed(range(pps)):
            g = gc_refs[p][...]
            within = g
            total = g
            s_ = heads
            while s_ < _LANES:
                within = within + jnp.where(lane < _LANES - s_, pltpu.roll(within, _LANES - s_, axis=1), 0.0)
                total = total + pltpu.roll(total, s_, axis=1)
                s_ *= 2
            below = total
            s_ = 1
            while s_ < grows:
                below = below + jnp.where(row < grows - s_, pltpu.roll(below, grows - s_, axis=0), 0.0)
                s_ *= 2
            carry = carry_ref[...]
            later = (within - g) + (below - total) + carry
            carry_ref[...] = carry + below[0:1, :]
            kb = kc_refs[p][...].astype(_BF16)
            vb = vc_refs[p][...].astype(_BF16)
            s_ref[...] = lax.dot_general(q_ref[...], kb, _NT, preferred_element_type=_F32)
            for r in range(grows):
                cols = slice(r * _LANES, (r + 1) * _LANES)
                s_ref[:, cols] = s_ref[:, cols] + later[r:r + 1, :]
            update(s_ref[...] + mask_ref[...], vb)

    @pl.when(step == pl.num_programs(1) - 1)
    def _():
        out = acc_ref[...] / l_ref[...]
        for h in range(heads):
            o_ref[:, h * dh:(h + 1) * dh] = out[h * nq:(h + 1) * nq, :].astype(o_ref.dtype)


def _fox_sample(q, knew, vnew, gnew, cache_k, cache_v, cache_logf, page_table, layer, nq, heads, dh):
    bsz, n_pages = page_table.shape
    ne, n_pool, page = cache_k.shape[:3]
    width = heads * dh
    hq = heads * nq
    assert nq * heads == _LANES and _LANES % heads == 0
    kh = page * heads
    ck = cache_k.reshape(ne, n_pool, kh, dh)
    cv = cache_v.reshape(ne, n_pool, kh, dh)
    cg = cache_logf.reshape(ne, n_pool, kh // _LANES, _LANES)
    pps = _tile(n_pages, 4, 1)
    nblk = n_pages // pps

    def new_map(b, s, pt):
        return (b, 0, 0)

    def page_map(p):
        def index(b, s, pt):
            return (layer, pt[b * n_pages + (nblk - jnp.maximum(s, 1)) * pps + p], 0, 0)
        return index

    wide = [pl.BlockSpec((None, None, kh, dh), page_map(p)) for p in range(pps)]
    narrow = [pl.BlockSpec((None, None, kh // _LANES, _LANES), page_map(p)) for p in range(pps)]
    grid_spec = pltpu.PrefetchScalarGridSpec(
        num_scalar_prefetch=1,
        grid=(bsz, nblk + 1),
        in_specs=[pl.BlockSpec((None, hq, dh), new_map),
                  pl.BlockSpec((hq, dh), lambda b, s, pt: (b, 0)),
                  pl.BlockSpec((hq, dh), lambda b, s, pt: (b, 0)),
                  pl.BlockSpec((None, 1, _LANES), new_map)] + wide + wide + narrow,
        out_specs=pl.BlockSpec((None, nq, width), new_map),
        scratch_shapes=[pltpu.VMEM((hq, 1), _F32), pltpu.VMEM((hq, 1), _F32),
                        pltpu.VMEM((hq, dh), _F32), pltpu.VMEM((1, _LANES), _F32),
                        pltpu.VMEM((hq, 1), _F32),
                        pltpu.VMEM((hq, kh), _F32), pltpu.VMEM((hq, kh), _F32)],
    )
    return pl.pallas_call(
        functools.partial(_fox_sample_body, heads=heads, nq=nq, dh=dh, pps=pps),
        grid_spec=grid_spec,
        out_shape=jax.ShapeDtypeStruct((bsz, nq, width), _BF16),
        compiler_params=_params("parallel", "arbitrary"),
        name="fox_sample",
    )(page_table.reshape(-1), q, knew, vnew, gnew, *([ck] * pps), *([cv] * pps), *([cg] * pps))


def _gelu_tanh(y):
    return 0.5 * y * (1.0 + jnp.tanh(math.sqrt(2.0 / math.pi) * (y + 0.044715 * (y * y * y))))


def _tile_rows(x, n):
    return jnp.broadcast_to(x[None], (n,) + x.shape).reshape(n * x.shape[0], x.shape[1])


def _repeat_rows(x, n):
    return jnp.broadcast_to(x[:, None, :], (x.shape[0], n, x.shape[1])).reshape(x.shape[0] * n, x.shape[1])


def _s5_chunk_operators(bbr, bbi, cr, ci, crt, cit, prv, piv, prt, pit, glu, *, ch, t_len, ns):
    tc = t_len * ch
    per = _LANES // ch
    bbr_t, bbi_t = _tile_rows(bbr, t_len), _tile_rows(bbi, t_len)
    pr_r, pi_r = _repeat_rows(prv, ch), _repeat_rows(piv, ch)
    x_re = bbr_t * pr_r - bbi_t * pi_r
    x_im = bbr_t * pi_r + bbi_t * pr_r
    ws = jnp.concatenate([x_re, x_im], axis=1)
    k_rep = (lax.dot_general(x_re, _tile_rows(cr, per), _NT, precision=_HI, preferred_element_type=_F32)
             - lax.dot_general(x_im, _tile_rows(ci, per), _NT, precision=_HI, preferred_element_type=_F32))
    k_ext = jnp.concatenate([k_rep, jnp.zeros((tc, _LANES), _F32)], axis=0)
    lane_blk = lax.broadcasted_iota(jnp.int32, (tc, _LANES), 1) // ch
    cols = []
    for col in range(tc // _LANES):
        acc = jnp.zeros((tc, _LANES), _F32)
        for tt in range(per):
            r0 = ch * (t_len - 1 - (col * per + tt))
            acc = jnp.where(lane_blk == tt, k_ext[r0:r0 + tc, :], acc)
        cols.append(acc)
    wy = cols[0] if len(cols) == 1 else jnp.concatenate(cols, axis=1)
    e_mat = (lax.broadcasted_iota(jnp.int32, (t_len, tc), 0)
             == lax.broadcasted_iota(jnp.int32, (t_len, tc), 1) // ch).astype(_F32)
    f_mat = (lax.broadcasted_iota(jnp.int32, (ch, tc), 0)
             == lax.broadcasted_iota(jnp.int32, (ch, tc), 1) % ch).astype(_F32)
    hdot = functools.partial(jnp.dot, precision=_HI, preferred_element_type=_F32)
    pr1, pi1 = hdot(prt, e_mat), hdot(pit, e_mat)
    crx, cix = hdot(crt, f_mat), hdot(cit, f_mat)
    wh = jnp.concatenate([pr1 * crx - pi1 * cix, -(pi1 * crx + pr1 * cix)], axis=0)
    same_t = (lax.broadcasted_iota(jnp.int32, (tc, tc), 0) // ch
              == lax.broadcasted_iota(jnp.int32, (tc, tc), 1) // ch)
    wg = jnp.where(same_t, _tile_rows(hdot(glu, f_mat), t_len), 0.0)
    return wy, ws, wh, wg


def _s5_body(z_ref, bbr_ref, bbi_ref, cr_ref, ci_ref, crt_ref, cit_ref, prv_ref, piv_ref, prt_ref, pit_ref,
             glu_ref, dv_ref, gb_ref, a1_ref, a2_ref, h0_ref,
             o_ref, hf_ref, pin_ref, rin_ref, zacc_ref, ybuf_ref, *, gpb, ch, t_len, bsz, nc, ns):
    tc = t_len * ch
    rows = bsz * nc
    per = _LANES // ch
    ncol = tc // _LANES
    for t in range(t_len):
        pin_ref[t] = z_ref[pl.ds(t, rows, stride=t_len), :]
    slot = lax.broadcasted_iota(jnp.int32, (rows, _LANES), 1) // ch
    for col in range(ncol):
        for d in range(per):
            acc = jnp.zeros((rows, _LANES), _F32)
            for tt in range(per):
                acc = jnp.where(slot == (tt - d) % per, pin_ref[col * per + tt], acc)
            rin_ref[col * per + d] = acc if d == 0 else pltpu.roll(acc, d * ch, axis=1)
    zacc_ref[...] = jnp.zeros_like(zacc_ref)

    def one(g, _):
        wy, ws, wh, wg = _s5_chunk_operators(
            bbr_ref[g], bbi_ref[g], cr_ref[g], ci_ref[g], crt_ref[g], cit_ref[g], prv_ref[g], piv_ref[g],
            prt_ref[g], pit_ref[g], glu_ref[g], ch=ch, t_len=t_len, ns=ns)
        cols = []
        for col in range(ncol):
            acc = jnp.zeros((rows, _LANES), _F32)
            for tt in range(per):
                acc = jnp.where(slot == tt, rin_ref[col * per + (tt - g + per) % per], acc)
            cols.append(acc)
        u = cols[0] if ncol == 1 else jnp.concatenate(cols, axis=1)
        ub = u.astype(_BF16)
        yin = jnp.dot(ub, wy.astype(_BF16), preferred_element_type=_F32)
        a1 = a1_ref[g]
        a2 = a2_ref[g]
        if nc == 1:
            st = jnp.dot(u, ws, precision=_HI, preferred_element_type=_F32)
            hstart = h0_ref[g]
            hf_ref[g] = a1[0:1] * hstart + a2[0:1] * pltpu.roll(hstart, ns, axis=1) + st
        else:
            x = jnp.dot(ub, ws.astype(_BF16), preferred_element_type=_F32)
            chunk = lax.broadcasted_iota(jnp.int32, (rows, 2 * ns), 0) % nc
            for k in range(nc.bit_length() - 1):
                sh = 1 << k
                xs = pltpu.roll(x, sh, axis=0)
                upd = a1[k:k + 1] * xs + a2[k:k + 1] * pltpu.roll(xs, ns, axis=1)
                x = x + jnp.where(chunk >= sh, upd, 0.0)
            hstart = jnp.where(chunk >= 1, pltpu.roll(x, 1, axis=0), 0.0)
            for b in range(bsz):
                hf_ref[g, b:b + 1, :] = x[(b + 1) * nc - 1:(b + 1) * nc, :]
        y = yin + jnp.dot(hstart.astype(_BF16), wh.astype(_BF16), preferred_element_type=_F32)
        y = _gelu_tanh(y + dv_ref[g] * u)
        gate = jax.nn.sigmoid(jnp.dot(y.astype(_BF16), wg.astype(_BF16),
                                      preferred_element_type=_F32) + gb_ref[g])
        out = y * gate
        for col in range(ncol):
            piece = out[:, col * _LANES:(col + 1) * _LANES]
            for d in range(per):
                zacc_ref[col * per + d] = jnp.where(slot == (g - d + per) % per, piece, zacc_ref[col * per + d])
        return 0

    lax.fori_loop(0, gpb, one, 0)
    for col in range(ncol):
        turned = [zacc_ref[col * per + d] if d == 0 else pltpu.roll(zacc_ref[col * per + d], d * ch, axis=1)
                  for d in range(per)]
        for tt in range(per):
            acc = jnp.zeros((rows, _LANES), _F32)
            for d in range(per):
                acc = jnp.where(slot == (tt + d) % per, turned[d], acc)
            ybuf_ref[pl.ds(col * per + tt, rows, stride=t_len), :] = acc
    o_ref[...] = ybuf_ref[...].astype(o_ref.dtype)


def _s5_operators(lam_re, lam_im, log_step, b_re, b_im, c_re, c_im, d, glu_w, glu_b, t_len, n_scan):
    groups, ns = lam_re.shape
    dt = jnp.exp(log_step)[:, None]

    def power(tau):
        tau = tau.astype(_F32)[None, :, None]
        mag = jnp.exp(lam_re[:, None, :] * dt[:, None, :] * tau)
        ang = lam_im[:, None, :] * dt[:, None, :] * tau
        return mag * jnp.cos(ang), mag * jnp.sin(ang)

    ab_re, ab_im = (p[:, 0] for p in power(jnp.ones((1,))))
    nr, ni = ab_re - 1.0, ab_im
    den = lam_re * lam_re + lam_im * lam_im
    f_re = (nr * lam_re + ni * lam_im) / den
    f_im = (ni * lam_re - nr * lam_im) / den
    bb_re = f_re[..., None] * b_re - f_im[..., None] * b_im
    bb_im = f_re[..., None] * b_im + f_im[..., None] * b_re
    pr, pi = power(jnp.arange(t_len + 1))
    rev = t_len - 1 - jnp.arange(t_len)
    dv = jnp.tile(d, (1, t_len))[:, None, :]
    gb = jnp.tile(glu_b, (1, t_len))[:, None, :]
    sr, si = power(t_len * (2 ** jnp.arange(_SUBLANES)))
    keep = (jnp.arange(_SUBLANES) < max(n_scan, 1))[None, :, None]
    sr, si = jnp.where(keep, sr, 0.0), jnp.where(keep, si, 0.0)
    a1 = jnp.concatenate([sr, sr], axis=-1)
    a2 = jnp.concatenate([-si, si], axis=-1)
    return (bb_re.transpose(0, 2, 1), bb_im.transpose(0, 2, 1), c_re, c_im,
            c_re.transpose(0, 2, 1), c_im.transpose(0, 2, 1), pr[:, rev], pi[:, rev],
            pr[:, 1:].transpose(0, 2, 1), pi[:, 1:].transpose(0, 2, 1), glu_w, dv, gb, a1, a2)


def _s5(u, h0, ops, bsz, seq, t_len):
    groups, ch, ns = ops[0].shape
    ns2 = 2 * ns
    tc = t_len * ch
    nc = seq // t_len
    rows = bsz * nc
    total = bsz * seq
    assert tc % _LANES == 0 and _LANES % ch == 0
    gpb = _LANES // ch
    if h0 is None:
        h0 = jnp.zeros((groups, bsz, ns2), _F32)

    def gmap(i):
        return (i, 0, 0)

    def per_group(a):
        return pl.BlockSpec((gpb,) + a.shape[1:], gmap)

    lane_block = pl.BlockSpec((total, _LANES), lambda i: (0, i))
    by_time = pltpu.VMEM((t_len, rows, _LANES), _F32)
    return pl.pallas_call(
        functools.partial(_s5_body, gpb=gpb, ch=ch, t_len=t_len, bsz=bsz, nc=nc, ns=ns),
        grid=(groups // gpb,),
        in_specs=[lane_block] + [per_group(a) for a in ops] + [per_group(h0)],
        out_specs=(lane_block, pl.BlockSpec((gpb, bsz, ns2), gmap)),
        out_shape=(jax.ShapeDtypeStruct((total, groups * ch), _BF16),
                   jax.ShapeDtypeStruct((groups, bsz, ns2), _F32)),
        scratch_shapes=[by_time, by_time, by_time, pltpu.VMEM((total, _LANES), _F32)],
        compiler_params=_params("arbitrary"),
        name="s5_chunks",
    )(u, *ops, h0)


def _conv_body(u_ref, halo_ref, st_ref, w_ref, wb_ref, lg_ref, lb_ref, o_ref, win_ref, sh_ref, y_ref,
               *, nblk, taps, rc, cc):
    i = pl.program_id(0)
    rows, d = u_ref.shape
    hb = halo_ref.shape[0]
    span = hb + rows
    first = (i % nblk) == 0

    @pl.when(first)
    def _():
        win_ref[0:hb, :] = st_ref[...]

    @pl.when(jnp.logical_not(first))
    def _():
        win_ref[0:hb, :] = halo_ref[...]

    win_ref[hb:span, :] = u_ref[...]
    win_ref[span:span + _SUBLANES, :] = jnp.zeros((_SUBLANES, d), _F32)
    off = hb - (taps - 1)

    def col_loop(ci, _):
        c0 = pl.multiple_of(ci * cc, cc)
        for r in range(_SUBLANES):
            sh_ref[r] = win_ref[pl.ds(r, span), pl.ds(c0, cc)]
        bias = wb_ref[:, pl.ds(c0, cc)]
        for r0 in range(0, rows, rc):
            acc = jnp.zeros((rc, cc), _F32) + bias
            for j in range(taps):
                a, r = divmod(off + j, _SUBLANES)
                acc = acc + sh_ref[r, pl.ds(r0 + _SUBLANES * a, rc), :] * w_ref[pl.ds(j, 1), pl.ds(c0, cc)]
            y_ref[r0:r0 + rc, pl.ds(c0, cc)] = acc
        return 0

    lax.fori_loop(0, d // cc, col_loop, 0)

    def norm_rows(ri, _):
        r0 = pl.multiple_of(ri * _SUBLANES, _SUBLANES)
        y = y_ref[pl.ds(r0, _SUBLANES), :]
        mu = jnp.mean(y, axis=-1, keepdims=True)
        yc = y - mu
        var = jnp.mean(yc * yc, axis=-1, keepdims=True)
        z = yc * lax.rsqrt(var + LN_EPS) * lg_ref[...] + lb_ref[...]
        y_ref[pl.ds(r0, _SUBLANES), :] = z * jax.nn.sigmoid(z)
        return 0

    lax.fori_loop(0, rows // _SUBLANES, norm_rows, 0)
    o_ref[...] = y_ref[...].astype(o_ref.dtype)


def _conv_ln_silu(u, state, dw, dw_b, ln_g, ln_b, seq):
    rows_total, d = u.shape
    taps = dw.shape[0]
    hb = -(-(taps - 1) // _SUBLANES) * _SUBLANES
    st = jnp.pad(state.astype(_F32), ((0, 0), (hb - (taps - 1), 0), (0, 0)))
    rows = _tile(seq, 256, hb) if seq % hb == 0 else seq
    nblk = seq // rows
    rc = _tile(rows, 64, _SUBLANES)
    cc = _tile(d, 256, _LANES)
    per = rows // hb if nblk > 1 else 1
    halo_src = u if nblk > 1 else st[0]

    def halo_map(i):
        return (jnp.maximum(i * per - 1, 0) if nblk > 1 else 0, 0)

    vec = pl.BlockSpec((1, d), lambda i: (0, 0))
    return pl.pallas_call(
        functools.partial(_conv_body, nblk=nblk, taps=taps, rc=rc, cc=cc),
        grid=(rows_total // rows,),
        in_specs=[pl.BlockSpec((rows, d), lambda i: (i, 0)),
                  pl.BlockSpec((hb, d), halo_map),
                  pl.BlockSpec((None, hb, d), lambda i: (i // nblk, 0, 0)),
                  pl.BlockSpec((taps, d), lambda i: (0, 0)),
                  vec, vec, vec],
        out_specs=pl.BlockSpec((rows, d), lambda i: (i, 0)),
        out_shape=jax.ShapeDtypeStruct((rows_total, d), _BF16),
        scratch_shapes=[pltpu.VMEM((hb + rows + _SUBLANES, d), _F32),
                        pltpu.VMEM((_SUBLANES, hb + rows, cc), _F32),
                        pltpu.VMEM((rows, d), _F32)],
        compiler_params=_params("parallel"),
        name="conv_ln_silu",
    )(u, halo_src, st, dw, dw_b.reshape(1, d), ln_g.reshape(1, d), ln_b.reshape(1, d))


def kernel(x_prompt, x_sample, cache_k, cache_v, cache_logf, state_s5_re, state_s5_im, state_conv, page_table, c_prompt, c_sample, mod_w, mod_b, norm_mix_pre, norm_mix_post, norm_ffn_pre, norm_ffn_post, ffn_w_gate, ffn_w_up, ffn_w_down, hyb_w_in, hyb_b_f, hyb_w_out, s5_lambda_re, s5_lambda_im, s5_log_step, s5_b_re, s5_b_im, s5_c_re, s5_c_im, s5_d, s5_glu_w, s5_glu_b, conv_w_in, conv_b_in, conv_dw, conv_dw_b, conv_ln_g, conv_ln_b, conv_w_out):
    bp, lp, d = x_prompt.shape
    bs, ls, _ = x_sample.shape
    depth = mod_w.shape[0]
    heads, dh = cache_k.shape[3], cache_k.shape[4]
    fw = heads * dh
    ns = s5_lambda_re.shape[2]
    s5w = s5_b_re.shape[1] * s5_b_re.shape[-1]
    taps = conv_dw.shape[1]
    hidden = ffn_w_gate.shape[2]
    seqs = ((bp, lp), (bs, ls))

    n_c = bp + bs
    c_rows = -(-n_c // (2 * _SUBLANES)) * (2 * _SUBLANES)
    c_all = jnp.pad(jnp.concatenate([c_prompt, c_sample], axis=0), ((0, c_rows - n_c), (0, 0)))
    mod = _modulation(c_all, mod_w, mod_b).reshape(depth, c_rows, 6, d)

    def mod_terms(i, grp):
        lo, hi = (0, bp) if grp == 0 else (bp, bp + bs)
        return [mod[i, lo:hi, t][:, None, :] for t in range(6)]

    xs = [x_prompt.reshape(bp * lp, d), x_sample.reshape(bs * ls, d)]
    terms = [mod_terms(0, g) for g in range(2)]
    hs = [_norm_mod(xs[g], norm_mix_pre[0], terms[g][1], terms[g][0], seqs[g][1]) for g in range(2)]

    outs = {k: ([], []) for k in ("k", "v", "f", "s5r", "s5i", "conv")}
    conv_bias = conv_b_in.reshape(conv_b_in.shape[0], 1, conv_b_in.shape[1])

    for i in range(depth):
        j = i // 2
        if i % 2 == 0:
            w_u = hyb_w_in[j][:, 3 * fw + heads:][None]
            w_f = jnp.pad(hyb_w_in[j][:, 3 * fw:3 * fw + heads], ((0, 0), (0, _LANES - heads)))[None]
            b_f = jnp.pad(hyb_b_f[j], (0, _LANES - heads)).reshape(1, 1, _LANES)
            s5_raw = (s5_lambda_re[j], s5_lambda_im[j], s5_log_step[j], s5_b_re[j], s5_b_im[j],
                      s5_c_re[j], s5_c_im[j], s5_d[j], s5_glu_w[j], s5_glu_b[j])
            mixed = []
            for g, (bsz, seq) in enumerate(seqs):
                q, k, v = _mm([hs[g]], [(hyb_w_in, j, 0)], [], 3 * fw, _ep_id, _F32, 1024, 512, "in_proj_qkv",
                              parts=3)
                uz = _mm([hs[g]], [(w_u, 0, 0)], [], s5w, _ep_id, _F32, 1024, 512, "in_proj_s5")
                logf = _mm([hs[g]], [(w_f, 0, 0)], [(b_f, 0, 0)], _LANES, _ep_logsigmoid, _F32, 1024, _LANES,
                           "forget_gate")
                k_out, v_out = _split_heads(k, heads, dh), _split_heads(v, heads, dh)
                outs["k"][g].append(k_out.reshape(bsz, seq, heads, dh))
                outs["v"][g].append(v_out.reshape(bsz, seq, heads, dh))
                outs["f"][g].append(logf[:, :heads].reshape(bsz, seq, heads))
                if g == 0:
                    cum = _cumsum_rows(logf, seq)
                    cum_heads = cum[:, :heads].reshape(bsz, seq, heads).transpose(0, 2, 1)
                    att = _fox_prompt(q, k, v, cum, cum_heads, bsz, seq, heads, dh)
                    t_len = _S5_CHUNK if seq % _S5_CHUNK == 0 else seq
                    h0 = None
                else:
                    qs = (q.reshape(bsz, seq, heads, dh) * (dh ** -0.5)).transpose(0, 2, 1, 3)
                    qs = qs.reshape(bsz, heads * seq, dh).astype(_BF16)
                    gnew = logf[:, :heads].reshape(bsz, 1, seq * heads)
                    att = _fox_sample(qs, k_out, v_out, gnew, cache_k, cache_v, cache_logf, page_table,
                                      j, seq, heads, dh).reshape(bsz * seq, fw)
                    t_len = seq
                    h0 = jnp.concatenate([state_s5_re[j], state_s5_im[j]], axis=-1).transpose(1, 0, 2)
                n_scan = (seq // t_len).bit_length() - 1
                ops = _s5_operators(*s5_raw, t_len, n_scan)
                ssm, hf = _s5(uz, h0, ops, bsz, seq, t_len)
                hf = hf.transpose(1, 0, 2)
                outs["s5r"][g].append(hf[..., :ns])
                outs["s5i"][g].append(hf[..., ns:])
                mixed.append(_mm([att, ssm], [(hyb_w_out, j, 0)], [], d, _ep_id, _F32, 1024, 512, "out_proj"))
        else:
            mixed = []
            for g, (bsz, seq) in enumerate(seqs):
                u = _mm([hs[g]], [(conv_w_in, j, 0), (conv_w_in, j, d)], [(conv_bias, j, 0), (conv_bias, j, d)],
                        d, _ep_glu, _F32, 1024, 256, "conv_in_glu")
                state = jnp.zeros((bsz, taps - 1, d), _F32) if g == 0 else state_conv[j]
                u3 = u.reshape(bsz, seq, d)
                if seq >= taps - 1:
                    outs["conv"][g].append(u3[:, seq - (taps - 1):])
                else:
                    outs["conv"][g].append(jnp.concatenate([state[:, seq:], u3], axis=1))
                y = _conv_ln_silu(u, state, conv_dw[j], conv_dw_b[j], conv_ln_g[j], conv_ln_b[j], seq)
                mixed.append(_mm([y], [(conv_w_out, j, 0)], [], d, _ep_id, _F32, 1024, 512, "conv_out"))

        nxt_terms = [mod_terms(i + 1, g) for g in range(2)] if i + 1 < depth else None
        w_down = _cast_bf16(ffn_w_down, i)
        for g, (bsz, seq) in enumerate(seqs):
            t = terms[g]
            x1, h2 = _res_norm(xs[g], mixed[g], norm_mix_post[i], t[2], (norm_ffn_pre[i], t[4], t[3]), seq)
            a = _mm([h2], [(ffn_w_gate, i, 0), (ffn_w_up, i, 0)], [], hidden, _ep_swiglu, _BF16, 1024, 256,
                    "ffn_gate_up")
            f = _mm([a], [(w_down, 0, 0)], [], d, _ep_id, _F32, 512, 256, "ffn_down")
            nxt = None if nxt_terms is None else (norm_mix_pre[i + 1], nxt_terms[g][1], nxt_terms[g][0])
            xs[g], hs[g] = _res_norm(x1, f, norm_ffn_post[i], t[5], nxt, seq)
        terms = nxt_terms

    def stack(key, g):
        parts = outs[key][g]
        return parts[0][None] if len(parts) == 1 else jnp.stack(parts)

    return (xs[0].reshape(bp, lp, d), xs[1].reshape(bs, ls, d),
            stack("k", 0), stack("v", 0), stack("f", 0),
            stack("k", 1), stack("v", 1), stack("f", 1),
            stack("s5r", 0), stack("s5i", 0), stack("s5r", 1), stack("s5i", 1),
            stack("conv", 0), stack("conv", 1))
```
